```python
import math
import jax, jax.numpy as jnp
from jax import lax
import numpy as np

D_MODEL = 2048
BATCH = 2
SEQ = 4096
DEPTH = 2
DEC_BATCH = 128
DEC_SEQ = 4
PAST_LEN = 8192
PAGE_SIZE = 128

HEAD_DIM = 128
GDN_HEADS = 12
GDN_DK = HEAD_DIM
GDN_DV = HEAD_DIM
GDN_CONV = 4
GDN_CHUNK = 64
SWA_Q_HEADS = 12
SWA_KV_HEADS = 4
SWA_GROUP = SWA_Q_HEADS // SWA_KV_HEADS
WINDOW = 128
SWA_BLOCK = 128
REL_BUCKETS = 32
REL_MAX_DIST = 128
MEM_TOKENS = 256
MEM_HEADS = 4
D_FF = 7168
N_EXPERTS = 8
TOP_K = 2
NORM_EPS = 1e-6

N_GDN_LAYERS = (DEPTH + 1) // 2
N_SWA_LAYERS = DEPTH // 2
GDN_QK = GDN_HEADS * GDN_DK
GDN_V = GDN_HEADS * GDN_DV
GDN_CONV_DIM = 2 * GDN_QK + GDN_V
XQ_DIM = MEM_HEADS * HEAD_DIM
GDN_IN = GDN_CONV_DIM + GDN_V + 2 * GDN_HEADS + XQ_DIM
SWA_Q = SWA_Q_HEADS * HEAD_DIM
SWA_KV = SWA_KV_HEADS * HEAD_DIM
SWA_IN = SWA_Q + 2 * SWA_KV + XQ_DIM
MIX_A = GDN_V + XQ_DIM
MIX_B = SWA_Q + XQ_DIM

kernel_name = 'hybrid_gdn_swa_mem_moe_step'

F32 = jnp.float32


def rms_norm(x, gain):
    xf = x.astype(F32)
    y = xf * lax.rsqrt(jnp.mean(xf * xf, axis=-1, keepdims=True) + NORM_EPS)
    return (y * gain.astype(F32)).astype(x.dtype)


def l2_norm(x):
    xf = x.astype(F32)
    return xf * lax.rsqrt(jnp.sum(xf * xf, axis=-1, keepdims=True) + NORM_EPS)


def t5_bucket(dist):
    n = np.maximum(dist, 0)
    max_exact = REL_BUCKETS // 2
    large = max_exact + (np.log(np.maximum(n, 1) / max_exact) / np.log(REL_MAX_DIST / max_exact)
                         * (REL_BUCKETS - max_exact)).astype(np.int64)
    large = np.minimum(large, REL_BUCKETS - 1)
    return np.where(n < max_exact, n, large).astype(np.int32)


def swiglu(h, w_gu, w_down):
    gu = h @ w_gu
    return (jax.nn.silu(gu[..., :D_FF]) * gu[..., D_FF:]) @ w_down


def moe_swiglu(h, router, w_gu, w_down):
    logits = jnp.einsum('nld,de->nle', h, router, preferred_element_type=F32)
    top_v, top_i = lax.top_k(logits, TOP_K)
    gate = jnp.sum(jax.nn.one_hot(top_i, N_EXPERTS, dtype=F32) * jax.nn.softmax(top_v, axis=-1)[..., None], axis=-2)
    out = jnp.zeros_like(h)
    for e in range(N_EXPERTS):
        out = out + swiglu(h, w_gu[e], w_down[e]) * gate[..., e:e + 1].astype(h.dtype)
    return out


def memory_kv(mem, mem_gain, w_mkv, k_gain):
    n, m, _ = mem.shape
    kv = rms_norm(mem, mem_gain) @ w_mkv
    k = rms_norm(kv[..., :XQ_DIM].reshape(n, m, MEM_HEADS, HEAD_DIM), k_gain)
    v = kv[..., XQ_DIM:].reshape(n, m, MEM_HEADS, HEAD_DIM)
    return k, v


def cross_attend(q_raw, q_gain, mk, mv):
    n, l, _ = q_raw.shape
    q = rms_norm(q_raw.reshape(n, l, MEM_HEADS, HEAD_DIM), q_gain)
    s = jnp.einsum('nlhd,nmhd->nhlm', q, mk.astype(q.dtype), preferred_element_type=F32) * HEAD_DIM ** -0.5
    p = jax.nn.softmax(s, axis=-1)
    o = jnp.einsum('nhlm,nmhd->nlhd', p.astype(q.dtype), mv.astype(q.dtype))
    return o.reshape(n, l, XQ_DIM)


def causal_conv_silu(u, buf, w):
    l = u.shape[1]
    up = jnp.concatenate([buf.astype(u.dtype), u], axis=1)
    y = up[:, 0:l] * w[0]
    for j in range(1, GDN_CONV):
        y = y + up[:, j:j + l] * w[j]
    return jax.nn.silu(y), up[:, l:]


def gated_delta_chunked(q, k, v, g, beta, s0):
    n, l, h, dk = q.shape
    dv = v.shape[-1]
    c = math.gcd(l, GDN_CHUNK)
    nc = l // c

    def blocks(t):
        t = t.reshape((n, nc, c, h) + t.shape[3:])
        return jnp.moveaxis(t, (1, 3), (0, 2))

    qb, kb, vb, gb, bb = blocks(q), blocks(k), blocks(v), blocks(g), blocks(beta)
    gc = jnp.cumsum(gb, axis=-1)
    idx = np.arange(c)
    lower = idx[:, None] >= idx[None, :]
    strict = idx[:, None] > idx[None, :]
    decay = jnp.exp(jnp.where(lower, gc[..., :, None] - gc[..., None, :], -jnp.inf))
    kbeta = kb * bb[..., None]
    a_mat = jnp.einsum('...id,...jd->...ij', kbeta, kb) * jnp.where(strict, decay, 0.0)
    lhs = a_mat + jnp.eye(c, dtype=F32)
    u = lax.linalg.triangular_solve(lhs, vb * bb[..., None], left_side=True, lower=True, unit_diagonal=True)
    w = lax.linalg.triangular_solve(lhs, kbeta * jnp.exp(gc)[..., None], left_side=True, lower=True, unit_diagonal=True)
    p_intra = jnp.einsum('...id,...jd->...ij', qb, kb) * decay
    qg = qb * jnp.exp(gc)[..., None]
    kg = kb * jnp.exp(gc[..., -1:] - gc)[..., None]
    g_last = jnp.exp(gc[..., -1])

    def step(s, xs):
        u_c, w_c, p_c, qg_c, kg_c, gl_c = xs
        v_new = u_c - jnp.einsum('nhcd,nhde->nhce', w_c, s)
        o_c = jnp.einsum('nhcd,nhde->nhce', qg_c, s) + jnp.einsum('nhij,nhje->nhie', p_c, v_new)
        s = s * gl_c[..., None, None] + jnp.einsum('nhcd,nhce->nhde', kg_c, v_new)
        return s, o_c

    s_fin, o = lax.scan(step, s0, (u, w, p_intra, qg, kg, g_last))
    o = jnp.moveaxis(o, (0, 2), (1, 3)).reshape(n, l, h, dv)
    return o, s_fin


def gdn_block(x, conv_buf, s0, mk, mv, norm_g, w_in, w_conv, a_log, dt_bias, o_gain, xq_gain, w_out):
    n, l, _ = x.shape
    proj = rms_norm(x, norm_g) @ w_in
    o1 = GDN_CONV_DIM
    o2 = o1 + GDN_V
    o3 = o2 + GDN_HEADS
    o4 = o3 + GDN_HEADS
    qkv, new_buf = causal_conv_silu(proj[..., :o1], conv_buf, w_conv)
    q = l2_norm(qkv[..., :GDN_QK].reshape(n, l, GDN_HEADS, GDN_DK)) * GDN_DK ** -0.5
    k = l2_norm(qkv[..., GDN_QK:2 * GDN_QK].reshape(n, l, GDN_HEADS, GDN_DK))
    v = qkv[..., 2 * GDN_QK:].reshape(n, l, GDN_HEADS, GDN_DV).astype(F32)
    z = proj[..., o1:o2].reshape(n, l, GDN_HEADS, GDN_DV).astype(F32)
    g = -jnp.exp(a_log.astype(F32)) * jax.nn.softplus(proj[..., o2:o3].astype(F32) + dt_bias.astype(F32))
    beta = jax.nn.sigmoid(proj[..., o3:o4].astype(F32))
    o, s_fin = gated_delta_chunked(q, k, v, g, beta, s0.astype(F32))
    o = rms_norm(o, o_gain) * jax.nn.silu(z)
    o_mem = cross_attend(proj[..., o4:], xq_gain, mk, mv)
    mixed = jnp.concatenate([o.reshape(n, l, GDN_V).astype(x.dtype), o_mem.astype(x.dtype)], axis=-1)
    return mixed @ w_out, new_buf, s_fin.astype(s0.dtype)


def sink_band_attention(qb, kb, vb, dist, valid, rel_bias, sinks):
    lq, lk = dist.shape
    bias = rel_bias.astype(F32)[t5_bucket(dist)]
    bias = jnp.transpose(bias.reshape(lq, lk, SWA_KV_HEADS, SWA_GROUP), (2, 3, 0, 1))
    s = jnp.einsum('nbqhgd,nbkhd->nbhgqk', qb, kb, preferred_element_type=F32) * HEAD_DIM ** -0.5 + bias
    s = jnp.where(valid[None, :, None, None], s, -jnp.inf)
    sink = sinks.astype(F32).reshape(SWA_KV_HEADS, SWA_GROUP)[:, :, None, None]
    m = jnp.maximum(jnp.max(s, axis=-1, keepdims=True), sink)
    p = jnp.exp(s - m)
    p = p / (jnp.sum(p, axis=-1, keepdims=True) + jnp.exp(sink - m))
    return jnp.einsum('nbhgqk,nbkhd->nbqhgd', p.astype(vb.dtype), vb)


def swa_block(x, k_buf, v_buf, mk, mv, norm_g, w_in, q_gain, k_gain, sinks, rel_bias, xq_gain, w_out):
    n, l, _ = x.shape
    proj = rms_norm(x, norm_g) @ w_in
    o1 = SWA_Q
    o2 = o1 + SWA_KV
    o3 = o2 + SWA_KV
    q = rms_norm(proj[..., :o1].reshape(n, l, SWA_Q_HEADS, HEAD_DIM), q_gain)
    k = rms_norm(proj[..., o1:o2].reshape(n, l, SWA_KV_HEADS, HEAD_DIM), k_gain)
    v = proj[..., o2:o3].reshape(n, l, SWA_KV_HEADS, HEAD_DIM)
    if k_buf is None:
        nb = l // SWA_BLOCK
        qb = q.reshape(n, nb, SWA_BLOCK, SWA_KV_HEADS, SWA_GROUP, HEAD_DIM)

        def band(t):
            prev = jnp.pad(t[:, :l - SWA_BLOCK], ((0, 0), (SWA_BLOCK, 0), (0, 0), (0, 0)))
            return jnp.concatenate([prev.reshape(n, nb, SWA_BLOCK, SWA_KV_HEADS, HEAD_DIM),
                                    t.reshape(n, nb, SWA_BLOCK, SWA_KV_HEADS, HEAD_DIM)], axis=2)

        kb, vb = band(k), band(v)
        qi = np.arange(SWA_BLOCK)[:, None]
        kj = np.arange(2 * SWA_BLOCK)[None, :]
        dist = SWA_BLOCK + qi - kj
        valid = (dist >= 0) & (dist < WINDOW) & ((np.arange(nb)[:, None, None] > 0) | (kj >= SWA_BLOCK))
        wl = min(WINDOW, l)
        new_k, new_v = k[:, l - wl:], v[:, l - wl:]
    else:
        wl = k_buf.shape[1]
        kc = jnp.concatenate([k_buf.astype(k.dtype), k], axis=1)
        vc = jnp.concatenate([v_buf.astype(v.dtype), v], axis=1)
        qb = q.reshape(n, 1, l, SWA_KV_HEADS, SWA_GROUP, HEAD_DIM)
        kb, vb = kc[:, None], vc[:, None]
        qi = np.arange(l)[:, None]
        kj = np.arange(wl + l)[None, :]
        dist = wl + qi - kj
        valid = ((dist >= 0) & (dist < WINDOW))[None]
        new_k, new_v = kc[:, l:], vc[:, l:]
    o = sink_band_attention(qb, kb, vb, dist, valid, rel_bias, sinks).reshape(n, l, SWA_Q)
    o_mem = cross_attend(proj[..., o3:], xq_gain, mk, mv)
    mixed = jnp.concatenate([o.astype(x.dtype), o_mem.astype(x.dtype)], axis=-1)
    return mixed @ w_out, new_k, new_v


def setup_inputs(seed: int = 0) -> dict:
    key = jax.random.key(seed)
    ks = list(jax.random.split(key, 48))

    def nrm(shape, scale):
        return jax.random.normal(ks.pop(), shape, F32) * scale

    def gain(shape):
        return 1.0 + nrm(shape, 0.05)

    na, nb = N_GDN_LAYERS, N_SWA_LAYERS
    wbuf = min(WINDOW, PAST_LEN)
    a_log = jnp.log(jax.random.uniform(ks.pop(), (na, GDN_HEADS), F32, 1.0, 16.0))
    dt = jnp.exp(jax.random.uniform(ks.pop(), (na, GDN_HEADS), F32, math.log(1e-3), math.log(1e-1)))
    dt_bias = dt + jnp.log(-jnp.expm1(-dt))
    return {
        'x_prompt': nrm((BATCH, SEQ, D_MODEL), 1.0),
        'x_sample': nrm((DEC_BATCH, DEC_SEQ, D_MODEL), 1.0),
        'state_gdn': nrm((na, DEC_BATCH, GDN_HEADS, GDN_DK, GDN_DV), 0.5),
        'state_gdn_conv': nrm((na, DEC_BATCH, GDN_CONV - 1, GDN_CONV_DIM), 1.0),
        'cache_swa_k': nrm((nb, DEC_BATCH, wbuf, SWA_KV_HEADS, HEAD_DIM), 1.0),
        'cache_swa_v': nrm((nb, DEC_BATCH, wbuf, SWA_KV_HEADS, HEAD_DIM), 1.0),
        'cache_mem_k': nrm((DEPTH, DEC_BATCH, MEM_TOKENS, MEM_HEADS, HEAD_DIM), 1.0),
        'cache_mem_v': nrm((DEPTH, DEC_BATCH, MEM_TOKENS, MEM_HEADS, HEAD_DIM), 1.0),
        'mem_prompt': nrm((BATCH, MEM_TOKENS, D_MODEL), 1.0),
        'rel_bias': nrm((REL_BUCKETS, SWA_Q_HEADS), 0.5),
        'a_norm': gain((na, D_MODEL)),
        'a_w_in': nrm((na, D_MODEL, GDN_IN), D_MODEL ** -0.5),
        'a_conv': nrm((na, GDN_CONV, GDN_CONV_DIM), GDN_CONV ** -0.5),
        'a_A_log': a_log,
        'a_dt_bias': dt_bias,
        'a_o_norm': gain((na, GDN_DV)),
        'a_w_out': nrm((na, MIX_A, D_MODEL), MIX_A ** -0.5),
        'b_norm': gain((nb, D_MODEL)),
        'b_w_in': nrm((nb, D_MODEL, SWA_IN), D_MODEL ** -0.5),
        'b_q_norm': gain((nb, HEAD_DIM)),
        'b_k_norm': gain((nb, HEAD_DIM)),
        'b_sinks': nrm((nb, SWA_Q_HEADS), 0.5),
        'b_w_out': nrm((nb, MIX_B, D_MODEL), MIX_B ** -0.5),
        'c_mem_norm': gain((DEPTH, D_MODEL)),
        'c_w_mkv': nrm((DEPTH, D_MODEL, 2 * XQ_DIM), D_MODEL ** -0.5),
        'c_q_norm': gain((DEPTH, HEAD_DIM)),
        'c_k_norm': gain((DEPTH, HEAD_DIM)),
        'f_norm': gain((DEPTH, D_MODEL)),
        'd_w_gate_up': nrm((na, D_MODEL, 2 * D_FF), D_MODEL ** -0.5),
        'd_w_down': nrm((na, D_FF, D_MODEL), D_FF ** -0.5),
        'm_router': nrm((nb, D_MODEL, N_EXPERTS), D_MODEL ** -0.5),
        'm_w_gate_up': nrm((nb, N_EXPERTS, D_MODEL, 2 * D_FF), D_MODEL ** -0.5),
        'm_w_down': nrm((nb, N_EXPERTS, D_FF, D_MODEL), D_FF ** -0.5),
    }


def reference(x_prompt, x_sample, state_gdn, state_gdn_conv, cache_swa_k, cache_swa_v, cache_mem_k, cache_mem_v,
              mem_prompt, rel_bias, a_norm, a_w_in, a_conv, a_A_log, a_dt_bias, a_o_norm, a_w_out,
              b_norm, b_w_in, b_q_norm, b_k_norm, b_sinks, b_w_out, c_mem_norm, c_w_mkv, c_q_norm, c_k_norm,
              f_norm, d_w_gate_up, d_w_down, m_router, m_w_gate_up, m_w_down):
    yp, ys = x_prompt, x_sample
    n_p = x_prompt.shape[0]
    gdn_s_p, gdn_s_s, conv_p, conv_s = [], [], [], []
    swk_p, swk_s, swv_p, swv_s = [], [], [], []
    memk_p, memv_p = [], []
    for i in range(DEPTH):
        j = i // 2
        mk_p, mv_p = memory_kv(mem_prompt, c_mem_norm[i], c_w_mkv[i], c_k_norm[i])
        memk_p.append(mk_p)
        memv_p.append(mv_p)
        mk_s, mv_s = cache_mem_k[i], cache_mem_v[i]
        if i % 2 == 0:
            prm = (a_norm[j], a_w_in[j], a_conv[j], a_A_log[j], a_dt_bias[j], a_o_norm[j], c_q_norm[i], a_w_out[j])
            zero_buf = jnp.zeros((n_p, GDN_CONV - 1, GDN_CONV_DIM), yp.dtype)
            zero_s = jnp.zeros((n_p, GDN_HEADS, GDN_DK, GDN_DV), yp.dtype)
            dp, cbp, sp = gdn_block(yp, zero_buf, zero_s, mk_p, mv_p, *prm)
            ds, cbs, ss = gdn_block(ys, state_gdn_conv[j], state_gdn[j], mk_s, mv_s, *prm)
            gdn_s_p.append(sp)
            gdn_s_s.append(ss)
            conv_p.append(cbp)
            conv_s.append(cbs)
        else:
            prm = (b_norm[j], b_w_in[j], b_q_norm[j], b_k_norm[j], b_sinks[j], rel_bias, c_q_norm[i], b_w_out[j])
            dp, kp, vp = swa_block(yp, None, None, mk_p, mv_p, *prm)
            ds, kn, vn = swa_block(ys, cache_swa_k[j], cache_swa_v[j], mk_s, mv_s, *prm)
            swk_p.append(kp)
            swk_s.append(kn)
            swv_p.append(vp)
            swv_s.append(vn)
        yp = yp + dp
        ys = ys + ds
        hp = rms_norm(yp, f_norm[i])
        hs = rms_norm(ys, f_norm[i])
        if i % 2 == 0:
            yp = yp + swiglu(hp, d_w_gate_up[j], d_w_down[j])
            ys = ys + swiglu(hs, d_w_gate_up[j], d_w_down[j])
        else:
            yp = yp + moe_swiglu(hp, m_router[j], m_w_gate_up[j], m_w_down[j])
            ys = ys + moe_swiglu(hs, m_router[j], m_w_gate_up[j], m_w_down[j])
    return (yp, ys, jnp.stack(gdn_s_p), jnp.stack(gdn_s_s), jnp.stack(conv_p), jnp.stack(conv_s),
            jnp.stack(swk_p), jnp.stack(swk_s), jnp.stack(swv_p), jnp.stack(swv_s),
            jnp.stack(memk_p), jnp.stack(memv_p))
```

```python
import functools
import math

import numpy as np
import jax
import jax.numpy as jnp
from jax import lax
from jax.experimental import pallas as pl
from jax.experimental.pallas import tpu as pltpu

F32 = jnp.float32
BF16 = jnp.bfloat16

D_MODEL = 2048
HEAD_DIM = 128
GDN_HEADS = 12
GDN_CONV = 4
GDN_CHUNK = 64
GDN_QK = GDN_HEADS * HEAD_DIM
GDN_CONV_DIM = 3 * GDN_QK
SWA_Q_HEADS = 12
SWA_KV_HEADS = 4
SWA_GROUP = SWA_Q_HEADS // SWA_KV_HEADS
WINDOW = 128
REL_BUCKETS = 32
REL_MAX_DIST = 128
MEM_HEADS = 4
XQ_DIM = MEM_HEADS * HEAD_DIM
D_FF = 7168
N_EXPERTS = 8
NORM_EPS = 1e-6
NEG_BIG = -1e30

V7X_VMEM_LIMIT_BYTES = 56 * 1024 * 1024
HIGHEST = lax.Precision.HIGHEST


def _params(*sem):
    return pltpu.CompilerParams(dimension_semantics=sem, vmem_limit_bytes=V7X_VMEM_LIMIT_BYTES)


def _tile(n, pref):
    if n <= pref:
        return n
    for t in range(pref, 7, -1):
        if n % t == 0 and t % 8 == 0:
            return t
    return n


def _silu(x):
    return x * (1.0 / (1.0 + jnp.exp(-x)))


def _rms(x, gain):
    return x * lax.rsqrt(jnp.mean(x * x, axis=-1, keepdims=True) + NORM_EPS) * gain


def _rmsnorm_kernel(x_ref, g_ref, o_ref):
    o_ref[...] = _rms(x_ref[...], g_ref[...]).astype(o_ref.dtype)


def rmsnorm_rows(x, gain, out_dtype=BF16, tm_pref=1088):
    m, k = x.shape
    tm = _tile(m, tm_pref)
    return pl.pallas_call(
        _rmsnorm_kernel,
        grid=(m // tm,),
        in_specs=[pl.BlockSpec((tm, k), lambda i: (i, 0)), pl.BlockSpec((1, k), lambda i: (0, 0))],
        out_specs=pl.BlockSpec((tm, k), lambda i: (i, 0)),
        out_shape=jax.ShapeDtypeStruct((m, k), out_dtype),
        compiler_params=_params("parallel"),
    )(x, gain.reshape(1, k))


def _mm_kernel(*refs, mode):
    if mode == "plain":
        a_ref, w_ref, o_ref, wb_ref = refs
    elif mode == "residual":
        a_ref, w_ref, r_ref, o_ref, wb_ref = refs
    else:
        a_ref, w_ref, w2_ref, o_ref, wb_ref, wb2_ref = refs

    @pl.when(pl.program_id(1) == 0)
    def _():
        wb_ref[...] = w_ref[...].astype(BF16)
        if mode == "swiglu":
            wb2_ref[...] = w2_ref[...].astype(BF16)

    a = a_ref[...]
    acc = jnp.dot(a, wb_ref[...], preferred_element_type=F32)
    if mode == "residual":
        acc = acc + r_ref[...]
    elif mode == "swiglu":
        up = jnp.dot(a, wb2_ref[...], preferred_element_type=F32)
        acc = _silu(acc) * up
    o_ref[...] = acc.astype(o_ref.dtype)


def matmul_ws(a, w, *, n_cols, tn, tm_pref, out_dtype=F32, residual=None, up_col_offset=None):
    m, k = a.shape
    assert w.shape[0] == k and n_cols % tn == 0
    tm = _tile(m, tm_pref)
    mode = "residual" if residual is not None else ("swiglu" if up_col_offset is not None else "plain")
    in_specs = [pl.BlockSpec((tm, k), lambda j, i: (i, 0)), pl.BlockSpec((k, tn), lambda j, i: (0, j))]
    args = [a, w]
    scratch = [pltpu.VMEM((k, tn), BF16)]
    if mode == "residual":
        in_specs.append(pl.BlockSpec((tm, tn), lambda j, i: (i, j)))
        args.append(residual)
    elif mode == "swiglu":
        assert up_col_offset % tn == 0
        off = up_col_offset // tn
        in_specs.append(pl.BlockSpec((k, tn), lambda j, i: (0, j + off)))
        args.append(w)
        scratch.append(pltpu.VMEM((k, tn), BF16))
    return pl.pallas_call(
        functools.partial(_mm_kernel, mode=mode),
        grid=(n_cols // tn, m // tm),
        in_specs=in_specs,
        out_specs=pl.BlockSpec((tm, tn), lambda j, i: (i, j)),
        out_shape=jax.ShapeDtypeStruct((m, n_cols), out_dtype),
        scratch_shapes=scratch,
        compiler_params=_params("parallel", "arbitrary"),
    )(*args)


def _dot(a, b, dims, precision=None):
    return lax.dot_general(a, b, (dims, ((), ())), precision=precision, preferred_element_type=F32)


_NN = ((1,), (0,))
_NT = ((1,), (1,))
_TN = ((0,), (0,))


def _bdot(a, b, dims):
    return _dot(a.astype(BF16), b.astype(BF16), dims)


def _conv_silu(u, hist, w, row_in_seq=None):
    rows, c = u.shape
    y = u * w[GDN_CONV - 1:GDN_CONV]
    for s in range(1, GDN_CONV):
        r = pltpu.roll(u, s, axis=0)
        if row_in_seq is None:
            rows8 = lax.broadcasted_iota(jnp.int32, (8, c), 0)
            head = jnp.where(rows8 < s, pltpu.roll(hist, s, axis=0), r[0:8])
            r = jnp.concatenate([head, r[8:]], axis=0)
        else:
            r = jnp.where(row_in_seq >= s, r, pltpu.roll(hist, rows - 4 + s, axis=0))
        y = y + r * w[GDN_CONV - 1 - s:GDN_CONV - s]
    return _silu(y)


def _l2n(x):
    return x * lax.rsqrt(jnp.sum(x * x, axis=-1, keepdims=True) + NORM_EPS)


def _gates(x, prm):
    a_log = prm[:, 0:1]
    dt = prm[:, 1:2]
    xa = x + dt
    softplus = jnp.maximum(xa, 0.0) + jnp.log(1.0 + jnp.exp(-jnp.abs(xa)))
    g = -jnp.exp(a_log) * softplus
    beta = 1.0 / (1.0 + jnp.exp(-x))
    return g[0:4], beta[4:8]


def _chunk_prep(q, k, v, g_row, b_row, blk):
    c = GDN_CHUNK
    lg = int(math.log2(blk))
    ii = lax.broadcasted_iota(jnp.int32, (c, c), 0)
    jj = lax.broadcasted_iota(jnp.int32, (c, c), 1)
    same = (ii >> lg) == (jj >> lg)
    lower = same & (ii >= jj)
    strict = same & (ii > jj)
    eye = ii == jj
    last = jj == (((ii >> lg) << lg) + (blk - 1))
    csum = jnp.where(same & (ii <= jj), 1.0, 0.0)
    gc_row = _dot(jnp.broadcast_to(g_row, (8, c)), csum, _NN, HIGHEST)[0:1]
    gcb = jnp.broadcast_to(gc_row, (c, c))
    gc_col = jnp.sum(jnp.where(eye, gcb, 0.0), axis=1, keepdims=True)
    gl_col = jnp.sum(jnp.where(last, gcb, 0.0), axis=1, keepdims=True)
    beta_col = jnp.sum(jnp.where(eye, jnp.broadcast_to(b_row, (c, c)), 0.0), axis=1, keepdims=True)
    decay = jnp.where(lower, jnp.exp(jnp.where(lower, gc_col - gcb, 0.0)), 0.0)
    kb = k * beta_col
    a = jnp.where(strict, _bdot(kb, k, _NT) * decay, 0.0)
    ident = jnp.where(eye, 1.0, 0.0)
    t = ident - a
    pw = a
    for _ in range(lg - 1):
        pw = _dot(pw, pw, _NN, HIGHEST)
        t = t + _dot(t, pw, _NN, HIGHEST)
    eg = jnp.exp(gc_col)
    uw = _dot(t, jnp.concatenate([v * beta_col, kb * eg], axis=1), _NN, HIGHEST)
    p = _bdot(q, k, _NT) * decay
    return uw[:, :HEAD_DIM], uw[:, HEAD_DIM:], p, q * eg, k * jnp.exp(gl_col - gc_col), jnp.exp(gl_col)


def _gdn_out(o, z, o_gain):
    return (_rms(o, o_gain) * _silu(z)).astype(BF16)


GDN_HG = 4
GDN_HW = GDN_HG * HEAD_DIM


def _gdn_prompt_kernel(q_ref, k_ref, v_ref, z_ref, gt_ref, prm_ref, wq_ref, wk_ref, wv_ref, og_ref,
                       o_ref, s_ref, state, hist, *, n_chunks):
    @pl.when(pl.program_id(2) == 0)
    def _():
        state[...] = jnp.zeros_like(state)
        hist[...] = jnp.zeros_like(hist)

    qkv = []
    for idx, (u_ref, w_ref) in enumerate(((q_ref, wq_ref), (k_ref, wk_ref), (v_ref, wv_ref))):
        u = u_ref[...]
        qkv.append(_conv_silu(u, hist[idx], w_ref[...]))
        hist[idx] = u[u.shape[0] - 8:]
    qc, kc, vc = qkv
    og = og_ref[...]
    prep = {}
    for c in range(n_chunks):
        rows = slice(c * GDN_CHUNK, (c + 1) * GDN_CHUNK)
        g4, b4 = _gates(gt_ref[c, 0], prm_ref[0])
        for hh in range(GDN_HG):
            cols = slice(hh * HEAD_DIM, (hh + 1) * HEAD_DIM)
            q = _l2n(qc[rows, cols]) * (HEAD_DIM ** -0.5)
            k = _l2n(kc[rows, cols])
            u, w, p, qg, kg, gl = _chunk_prep(q, k, vc[rows, cols], g4[hh:hh + 1], b4[hh:hh + 1], GDN_CHUNK)
            wu = jnp.concatenate([w, u], axis=1)
            kwu = _bdot(kg, wu, _TN)
            pwu = _bdot(p, wu, _NN)
            prep[c, hh] = (qg - pwu[:, :HEAD_DIM], pwu[:, HEAD_DIM:], kwu[:, :HEAD_DIM], kwu[:, HEAD_DIM:],
                           gl[0:1, 0:1])
    for hh in range(GDN_HG):
        cols = slice(hh * HEAD_DIM, (hh + 1) * HEAD_DIM)
        s = state[hh]
        for c in range(n_chunks):
            rows = slice(c * GDN_CHUNK, (c + 1) * GDN_CHUNK)
            qp, op, mneg, bmat, gl = prep[c, hh]
            sb = s.astype(BF16)
            o = _dot(qp.astype(BF16), sb, _NN) + op
            s = gl * s - _dot(mneg.astype(BF16), sb, _NN) + bmat
            o_ref[rows, cols] = _gdn_out(o, z_ref[rows, cols], og)
        state[hh] = s
        s_ref[0, hh] = s


def gdn_prompt(proj, gt, prm, conv_w, o_gain, n_seq, seq_len, lb=256):
    nlb = seq_len // lb
    n_chunks = lb // GDN_CHUNK
    nq = GDN_QK // GDN_HW

    def rowcol(seg):
        return pl.BlockSpec((lb, GDN_HW), lambda n, g, l: (n * nlb + l, seg * nq + g))

    def wspec(seg):
        return pl.BlockSpec((GDN_CONV, GDN_HW), lambda n, g, l: (0, seg * nq + g))

    return pl.pallas_call(
        functools.partial(_gdn_prompt_kernel, n_chunks=n_chunks),
        grid=(n_seq, nq, nlb),
        in_specs=[rowcol(0), rowcol(1), rowcol(2), rowcol(3),
                  pl.BlockSpec((n_chunks, 1, 8, GDN_CHUNK), lambda n, g, l: (n * nlb + l, g, 0, 0)),
                  pl.BlockSpec((1, 8, 2), lambda n, g, l: (g, 0, 0)),
                  wspec(0), wspec(1), wspec(2),
                  pl.BlockSpec((1, HEAD_DIM), lambda n, g, l: (0, 0))],
        out_specs=[pl.BlockSpec((lb, GDN_HW), lambda n, g, l: (n * nlb + l, g)),
                   pl.BlockSpec((1, GDN_HG, HEAD_DIM, HEAD_DIM), lambda n, g, l: (n, g, 0, 0))],
        out_shape=[jax.ShapeDtypeStruct((n_seq * seq_len, GDN_QK), BF16),
                   jax.ShapeDtypeStruct((n_seq, GDN_HEADS, HEAD_DIM, HEAD_DIM), F32)],
        scratch_shapes=[pltpu.VMEM((GDN_HG, HEAD_DIM, HEAD_DIM), F32), pltpu.VMEM((3, 8, GDN_HW), F32)],
        compiler_params=_params("parallel", "parallel", "arbitrary"),
    )(proj, proj, proj, proj, gt, prm, conv_w, conv_w, conv_w, o_gain.reshape(1, HEAD_DIM))


def gdn_gate_params(a_log, dt_bias):
    p = jnp.stack([a_log.astype(F32), dt_bias.astype(F32)], axis=-1).reshape(GDN_HEADS // GDN_HG, GDN_HG, 2)
    return jnp.concatenate([p, jnp.zeros_like(p)], axis=1)


GDN_SEQ_PER_STEP = GDN_CHUNK // 4


def _gdn_sample_kernel(q_ref, k_ref, v_ref, z_ref, gt_ref, prm_ref, wq_ref, wk_ref, wv_ref, og_ref,
                       hq_ref, hk_ref, hv_ref, s0_ref, o_ref, s_ref):
    rows = GDN_CHUNK
    row_in_seq = lax.broadcasted_iota(jnp.int32, (rows, GDN_HW), 0) & 3
    qc = _conv_silu(q_ref[...], hq_ref[...], wq_ref[...], row_in_seq)
    kc = _conv_silu(k_ref[...], hk_ref[...], wk_ref[...], row_in_seq)
    vc = _conv_silu(v_ref[...], hv_ref[...], wv_ref[...], row_in_seq)
    g4, b4 = _gates(gt_ref[0, 0], prm_ref[0])
    og = og_ref[...]
    seq_of_row = lax.broadcasted_iota(jnp.int32, (rows, 1), 0) >> 2
    for hh in range(GDN_HG):
        cols = slice(hh * HEAD_DIM, (hh + 1) * HEAD_DIM)
        q = _l2n(qc[:, cols]) * (HEAD_DIM ** -0.5)
        k = _l2n(kc[:, cols])
        u, w, p, qg, kg, gl = _chunk_prep(q, k, vc[:, cols], g4[hh:hh + 1], b4[hh:hh + 1], 4)
        lhs = jnp.concatenate([w, qg], axis=0).astype(BF16)
        ws = jnp.zeros((rows, HEAD_DIM), F32)
        qs = jnp.zeros((rows, HEAD_DIM), F32)
        for b in range(GDN_SEQ_PER_STEP):
            r = _dot(lhs, s0_ref[b, hh].astype(BF16), _NN)
            sel = seq_of_row == b
            ws = jnp.where(sel, r[:rows], ws)
            qs = jnp.where(sel, r[rows:], qs)
        v_new = u - ws
        o = qs + _bdot(p, v_new, _NN)
        o_ref[:, cols] = _gdn_out(o, z_ref[:, cols], og)
        v_new_b = v_new.astype(BF16)
        for b in range(GDN_SEQ_PER_STEP):
            kgm = jnp.where(seq_of_row == b, kg, 0.0).astype(BF16)
            s_ref[b, hh] = s0_ref[b, hh] * gl[4 * b:4 * b + 1, 0:1] + _dot(kgm, v_new_b, _TN)


def gdn_sample(proj, row_off, gt, prm, conv_w, o_gain, conv_buf, s0):
    nb = s0.shape[0]
    assert nb % GDN_SEQ_PER_STEP == 0 and row_off % GDN_CHUNK == 0
    rb = row_off // GDN_CHUNK
    nq = GDN_QK // GDN_HW
    hist = jnp.pad(conv_buf.astype(F32), ((0, 0), (1, 0), (0, 0))).reshape(nb * 4, GDN_CONV_DIM)

    def rowcol(seg):
        return pl.BlockSpec((GDN_CHUNK, GDN_HW), lambda b, g: (rb + b, seg * nq + g))

    def hspec(seg):
        return pl.BlockSpec((GDN_CHUNK, GDN_HW), lambda b, g: (b, seg * nq + g))

    def wspec(seg):
        return pl.BlockSpec((GDN_CONV, GDN_HW), lambda b, g: (0, seg * nq + g))

    sspec = pl.BlockSpec((GDN_SEQ_PER_STEP, GDN_HG, HEAD_DIM, HEAD_DIM), lambda b, g: (b, g, 0, 0))
    return pl.pallas_call(
        _gdn_sample_kernel,
        grid=(nb // GDN_SEQ_PER_STEP, nq),
        in_specs=[rowcol(0), rowcol(1), rowcol(2), rowcol(3),
                  pl.BlockSpec((1, 1, 8, GDN_CHUNK), lambda b, g: (rb + b, g, 0, 0)),
                  pl.BlockSpec((1, 8, 2), lambda b, g: (g, 0, 0)),
                  wspec(0), wspec(1), wspec(2),
                  pl.BlockSpec((1, HEAD_DIM), lambda b, g: (0, 0)),
                  hspec(0), hspec(1), hspec(2), sspec],
        out_specs=[pl.BlockSpec((GDN_CHUNK, GDN_HW), lambda b, g: (b, g)), sspec],
        out_shape=[jax.ShapeDtypeStruct((nb * 4, GDN_QK), BF16),
                   jax.ShapeDtypeStruct(s0.shape, F32)],
        compiler_params=_params("parallel", "parallel"),
    )(proj, proj, proj, proj, gt, prm, conv_w, conv_w, conv_w, o_gain.reshape(1, HEAD_DIM),
      hist, hist, hist, s0.astype(F32))


def _softmax_pv(s, v):
    m = jnp.max(s, axis=-1, keepdims=True)
    p = jnp.exp(s - m)
    den = jnp.sum(p, axis=-1, keepdims=True)
    return _bdot(p, v, _NN) / den


def _xattn_prompt_kernel(q_ref, k_ref, v_ref, g_ref, o_ref):
    gain = g_ref[...]
    for h in range(MEM_HEADS):
        cols = slice(h * HEAD_DIM, (h + 1) * HEAD_DIM)
        q = _rms(q_ref[:, cols], gain)
        s = _bdot(q, k_ref[0, :, cols], _NT) * (HEAD_DIM ** -0.5)
        o_ref[:, cols] = _softmax_pv(s, v_ref[0, :, cols]).astype(o_ref.dtype)


def xattn_prompt(qsrc, col_blk, mk, mv, q_gain, n_seq, seq_len, tq=512):
    nq = seq_len // tq
    m = mk.shape[1]
    return pl.pallas_call(
        _xattn_prompt_kernel,
        grid=(n_seq, nq),
        in_specs=[pl.BlockSpec((tq, XQ_DIM), lambda n, i: (n * nq + i, col_blk)),
                  pl.BlockSpec((1, m, XQ_DIM), lambda n, i: (n, 0, 0)),
                  pl.BlockSpec((1, m, XQ_DIM), lambda n, i: (n, 0, 0)),
                  pl.BlockSpec((1, HEAD_DIM), lambda n, i: (0, 0))],
        out_specs=pl.BlockSpec((tq, XQ_DIM), lambda n, i: (n * nq + i, 0)),
        out_shape=jax.ShapeDtypeStruct((n_seq * seq_len, XQ_DIM), BF16),
        compiler_params=_params("parallel", "parallel"),
    )(qsrc, mk, mv, q_gain.reshape(1, HEAD_DIM))


SEQ_PER_STEP = 8
ROWS_PER_STEP = 4 * SEQ_PER_STEP


def _xattn_sample_kernel(q_ref, k_ref, v_ref, g_ref, o_ref):
    gain = g_ref[...]
    seq_of_row = lax.broadcasted_iota(jnp.int32, (ROWS_PER_STEP, 1), 0) >> 2
    for h in range(MEM_HEADS):
        cols = slice(h * HEAD_DIM, (h + 1) * HEAD_DIM)
        q = _rms(q_ref[:, cols], gain).astype(BF16)
        s = jnp.zeros((ROWS_PER_STEP, k_ref.shape[1]), F32)
        for b in range(SEQ_PER_STEP):
            s = jnp.where(seq_of_row == b, _dot(q, k_ref[b, :, cols].astype(BF16), _NT), s)
        s = s * (HEAD_DIM ** -0.5)
        p = jnp.exp(s - jnp.max(s, axis=-1, keepdims=True))
        den = jnp.sum(p, axis=-1, keepdims=True)
        pb = p.astype(BF16)
        o = jnp.zeros((ROWS_PER_STEP, HEAD_DIM), F32)
        for b in range(SEQ_PER_STEP):
            o = jnp.where(seq_of_row == b, _dot(pb, v_ref[b, :, cols].astype(BF16), _NN), o)
        o_ref[:, cols] = (o / den).astype(o_ref.dtype)


def xattn_sample(qsrc, row_off, col_blk, mk, mv, q_gain):
    nb, m, _ = mk.shape
    rb = row_off // ROWS_PER_STEP
    kv = pl.BlockSpec((SEQ_PER_STEP, m, XQ_DIM), lambda b: (b, 0, 0))
    return pl.pallas_call(
        _xattn_sample_kernel,
        grid=(nb // SEQ_PER_STEP,),
        in_specs=[pl.BlockSpec((ROWS_PER_STEP, XQ_DIM), lambda b: (rb + b, col_blk)), kv, kv,
                  pl.BlockSpec((1, HEAD_DIM), lambda b: (0, 0))],
        out_specs=pl.BlockSpec((ROWS_PER_STEP, XQ_DIM), lambda b: (b, 0)),
        out_shape=jax.ShapeDtypeStruct((nb * 4, XQ_DIM), BF16),
        compiler_params=_params("parallel"),
    )(qsrc, mk, mv, q_gain.reshape(1, HEAD_DIM))


def _mem_kv_split_kernel(kv_ref, g_ref, k_ref, v_ref):
    gain = g_ref[...]
    for h in range(MEM_HEADS):
        cols = slice(h * HEAD_DIM, (h + 1) * HEAD_DIM)
        k_ref[:, cols] = _rms(kv_ref[:, cols], gain)
    v_ref[...] = kv_ref[:, XQ_DIM:]


def mem_kv_split(kv, k_gain):
    rows = kv.shape[0]
    return pl.pallas_call(
        _mem_kv_split_kernel,
        grid=(1,),
        in_specs=[pl.BlockSpec((rows, 2 * XQ_DIM), lambda i: (0, 0)), pl.BlockSpec((1, HEAD_DIM), lambda i: (0, 0))],
        out_specs=[pl.BlockSpec((rows, XQ_DIM), lambda i: (0, 0))] * 2,
        out_shape=[jax.ShapeDtypeStruct((rows, XQ_DIM), F32)] * 2,
        compiler_params=_params("arbitrary"),
    )(kv, k_gain.reshape(1, HEAD_DIM))


def _t5_bucket(dist):
    n = np.maximum(dist, 0)
    max_exact = REL_BUCKETS // 2
    large = max_exact + (np.log(np.maximum(n, 1) / max_exact) / np.log(REL_MAX_DIST / max_exact)
                         * (REL_BUCKETS - max_exact)).astype(np.int64)
    large = np.minimum(large, REL_BUCKETS - 1)
    return np.where(n < max_exact, n, large).astype(np.int32)


def _bias_table(rel_bias, dist, valid):
    b = jnp.transpose(rel_bias.astype(F32)[_t5_bucket(dist)], (2, 0, 1))
    return jnp.where(jnp.asarray(valid)[None], b, NEG_BIG)


def _sink_softmax_pv(s, sink, v_parts):
    m = sink
    for si in s:
        m = jnp.maximum(m, jnp.max(si, axis=-1, keepdims=True))
    den = jnp.exp(sink - m)
    o = None
    for si, vi in zip(s, v_parts):
        p = jnp.exp(si - m)
        den = den + jnp.sum(p, axis=-1, keepdims=True)
        pv = vi(p.astype(BF16))
        o = pv if o is None else o + pv
    return o / den


def _swa_prompt_kernel(q_ref, kc_ref, kp_ref, vc_ref, vp_ref, bias_ref, qg_ref, kg_ref, sink_ref,
                       o_ref, kn_ref):
    first = pl.program_id(1) == 0
    qg = qg_ref[...]
    kg = kg_ref[...]
    blk = q_ref.shape[0]
    prev_col = lax.broadcasted_iota(jnp.int32, (SWA_GROUP * blk, 2 * blk), 1) < blk
    for g in range(SWA_KV_HEADS):
        kcols = slice(g * HEAD_DIM, (g + 1) * HEAD_DIM)
        k_cur = _rms(kc_ref[:, kcols], kg)
        kn_ref[0, :, kcols] = k_cur
        k_all = jnp.concatenate([_rms(kp_ref[:, kcols], kg), k_cur], axis=0)
        v_all = jnp.concatenate([vp_ref[:, kcols], vc_ref[:, kcols]], axis=0).astype(BF16)
        heads = range(g * SWA_GROUP, (g + 1) * SWA_GROUP)
        q = jnp.concatenate([_rms(q_ref[:, h * HEAD_DIM:(h + 1) * HEAD_DIM], qg) for h in heads], axis=0)
        s = _bdot(q, k_all, _NT) * (HEAD_DIM ** -0.5)
        s = s + jnp.concatenate([bias_ref[h] for h in heads], axis=0)
        s = jnp.where(prev_col & first, NEG_BIG, s)
        sink = jnp.concatenate([jnp.broadcast_to(sink_ref[0:1, h:h + 1], (blk, 1)) for h in heads], axis=0)
        o = _sink_softmax_pv([s], sink, [lambda p: _dot(p, v_all, _NN)])
        for i, h in enumerate(heads):
            o_ref[:, h * HEAD_DIM:(h + 1) * HEAD_DIM] = o[i * blk:(i + 1) * blk].astype(o_ref.dtype)


def swa_prompt(proj, rel_bias, q_gain, k_gain, sinks, n_seq, seq_len):
    blk = WINDOW
    nb = seq_len // blk
    qi = np.arange(blk)[:, None]
    kj = np.arange(2 * blk)[None, :]
    dist = blk + qi - kj
    bias = _bias_table(rel_bias, dist, (dist >= 0) & (dist < WINDOW))
    kv_w = SWA_KV_HEADS * HEAD_DIM
    kblk = (SWA_Q_HEADS * HEAD_DIM) // kv_w

    def cur(seg):
        return pl.BlockSpec((blk, kv_w), lambda n, i: (n * nb + i, kblk + seg))

    def prev(seg):
        return pl.BlockSpec((blk, kv_w), lambda n, i: (n * nb + jnp.maximum(i - 1, 0), kblk + seg))

    vec = pl.BlockSpec((1, HEAD_DIM), lambda n, i: (0, 0))
    sink_row = jnp.zeros((1, HEAD_DIM), F32).at[0, :SWA_Q_HEADS].set(sinks.astype(F32))
    return pl.pallas_call(
        _swa_prompt_kernel,
        grid=(n_seq, nb),
        in_specs=[pl.BlockSpec((blk, SWA_Q_HEADS * HEAD_DIM), lambda n, i: (n * nb + i, 0)),
                  cur(0), prev(0), cur(1), prev(1),
                  pl.BlockSpec((SWA_Q_HEADS, blk, 2 * blk), lambda n, i: (0, 0, 0)),
                  vec, vec, vec],
        out_specs=[pl.BlockSpec((blk, SWA_Q_HEADS * HEAD_DIM), lambda n, i: (n * nb + i, 0)),
                   pl.BlockSpec((1, blk, kv_w), lambda n, i: (n, 0, 0))],
        out_shape=[jax.ShapeDtypeStruct((n_seq * seq_len, SWA_Q_HEADS * HEAD_DIM), BF16),
                   jax.ShapeDtypeStruct((n_seq, blk, kv_w), F32)],
        compiler_params=_params("parallel", "arbitrary"),
    )(proj, proj, proj, proj, proj, bias, q_gain.reshape(1, HEAD_DIM), k_gain.reshape(1, HEAD_DIM), sink_row)


def _shift_in(cache, new_pad, b):
    w = cache.shape[0]
    rows = lax.broadcasted_iota(jnp.int32, cache.shape, 0)
    return jnp.where(rows >= w - 4, pltpu.roll(new_pad, w - 4 - 4 * b, axis=0), pltpu.roll(cache, w - 4, axis=0))


def _swa_sample_kernel(q_ref, kn_ref, vn_ref, kc_ref, vc_ref, bo_ref, bn_ref, qg_ref, kg_ref, sink_ref,
                       o_ref, ko_ref, vo_ref):
    qg = qg_ref[...]
    kg = kg_ref[...]
    rows = ROWS_PER_STEP
    w = kc_ref.shape[1]
    seq_of_row = (lax.broadcasted_iota(jnp.int32, (SWA_GROUP * rows, 1), 0) & (rows - 1)) >> 2
    k_new = jnp.concatenate([_rms(kn_ref[:, g * HEAD_DIM:(g + 1) * HEAD_DIM], kg) for g in range(SWA_KV_HEADS)],
                            axis=1)
    v_new = vn_ref[...]
    pad = jnp.zeros((w - rows, k_new.shape[1]), F32)
    k_pad = jnp.concatenate([k_new, pad], axis=0)
    v_pad = jnp.concatenate([v_new, pad], axis=0)
    for b in range(SEQ_PER_STEP):
        ko_ref[b] = _shift_in(kc_ref[b], k_pad, b)
        vo_ref[b] = _shift_in(vc_ref[b], v_pad, b)
    for g in range(SWA_KV_HEADS):
        kcols = slice(g * HEAD_DIM, (g + 1) * HEAD_DIM)
        heads = range(g * SWA_GROUP, (g + 1) * SWA_GROUP)
        q = jnp.concatenate([_rms(q_ref[:, h * HEAD_DIM:(h + 1) * HEAD_DIM], qg) for h in heads],
                            axis=0).astype(BF16)
        s_old = jnp.zeros((SWA_GROUP * rows, w), F32)
        for b in range(SEQ_PER_STEP):
            s_old = jnp.where(seq_of_row == b, _dot(q, kc_ref[b, :, kcols].astype(BF16), _NT), s_old)
        s_old = s_old * (HEAD_DIM ** -0.5) + bo_ref[g]
        s_new = _dot(q, k_new[:, kcols].astype(BF16), _NT) * (HEAD_DIM ** -0.5) + bn_ref[g]
        sink = jnp.concatenate([jnp.broadcast_to(sink_ref[0:1, h:h + 1], (rows, 1)) for h in heads], axis=0)

        def pv_old(p, kcols=kcols):
            o = jnp.zeros((SWA_GROUP * rows, HEAD_DIM), F32)
            for b in range(SEQ_PER_STEP):
                o = jnp.where(seq_of_row == b, _dot(p, vc_ref[b, :, kcols].astype(BF16), _NN), o)
            return o

        def pv_new(p, kcols=kcols):
            return _dot(p, v_new[:, kcols].astype(BF16), _NN)

        o = _sink_softmax_pv([s_old, s_new], sink, [pv_old, pv_new])
        for i, h in enumerate(heads):
            o_ref[:, h * HEAD_DIM:(h + 1) * HEAD_DIM] = o[i * rows:(i + 1) * rows].astype(o_ref.dtype)


def swa_sample(proj, row_off, k_cache, v_cache, rel_bias, q_gain, k_gain, sinks):
    nb, w, kv_w = k_cache.shape
    rows = ROWS_PER_STEP
    rb = row_off // rows
    kblk = (SWA_Q_HEADS * HEAD_DIM) // kv_w
    qi = np.arange(4)[:, None]
    dist_old = w + qi - np.arange(w)[None, :]
    b_old = _bias_table(rel_bias, dist_old, (dist_old >= 0) & (dist_old < WINDOW))
    dist_new = qi - np.arange(4)[None, :]
    b_new = _bias_table(rel_bias, dist_new, dist_new >= 0)
    b_old = jnp.broadcast_to(b_old.reshape(SWA_KV_HEADS, SWA_GROUP, 1, 4, w),
                             (SWA_KV_HEADS, SWA_GROUP, SEQ_PER_STEP, 4, w)).reshape(SWA_KV_HEADS, SWA_GROUP * rows, w)
    same_seq = np.eye(SEQ_PER_STEP, dtype=bool)[None, None, :, None, :, None]
    b_new = jnp.where(same_seq, b_new.reshape(SWA_KV_HEADS, SWA_GROUP, 1, 4, 1, 4), NEG_BIG)
    b_new = b_new.reshape(SWA_KV_HEADS, SWA_GROUP * rows, rows)
    vec = pl.BlockSpec((1, HEAD_DIM), lambda b: (0, 0))
    sink_row = jnp.zeros((1, HEAD_DIM), F32).at[0, :SWA_Q_HEADS].set(sinks.astype(F32))
    cache = pl.BlockSpec((SEQ_PER_STEP, w, kv_w), lambda b: (b, 0, 0))
    return pl.pallas_call(
        _swa_sample_kernel,
        grid=(nb // SEQ_PER_STEP,),
        in_specs=[pl.BlockSpec((rows, SWA_Q_HEADS * HEAD_DIM), lambda b: (rb + b, 0)),
                  pl.BlockSpec((rows, kv_w), lambda b: (rb + b, kblk)),
                  pl.BlockSpec((rows, kv_w), lambda b: (rb + b, kblk + 1)),
                  cache, cache,
                  pl.BlockSpec(b_old.shape, lambda b: (0, 0, 0)),
                  pl.BlockSpec(b_new.shape, lambda b: (0, 0, 0)),
                  vec, vec, vec],
        out_specs=[pl.BlockSpec((rows, SWA_Q_HEADS * HEAD_DIM), lambda b: (b, 0)), cache, cache],
        out_shape=[jax.ShapeDtypeStruct((nb * 4, SWA_Q_HEADS * HEAD_DIM), BF16),
                   jax.ShapeDtypeStruct(k_cache.shape, F32), jax.ShapeDtypeStruct(v_cache.shape, F32)],
        compiler_params=_params("parallel"),
    )(proj, proj, proj, k_cache, v_cache, b_old, b_new,
      q_gain.reshape(1, HEAD_DIM), k_gain.reshape(1, HEAD_DIM), sink_row)


MOE_TM = 512
PACK_W = D_MODEL // 2
HI_MASK = np.uint32(0xFFFF0000)


def _router_kernel(y_ref, g_ref, wr_ref, hp_ref, idx_ref, gate_ref):
    h = _rms(y_ref[...], g_ref[...])
    logits = _dot(h, wr_ref[...], _NN, HIGHEST)
    lane = lax.broadcasted_iota(jnp.int32, logits.shape, 1)
    l1 = jnp.where(lane < N_EXPERTS, logits, NEG_BIG)
    m1 = jnp.max(l1, axis=1, keepdims=True)
    i1 = jnp.min(jnp.where(l1 == m1, lane, HEAD_DIM), axis=1, keepdims=True)
    l2 = jnp.where(lane == i1, NEG_BIG, l1)
    m2 = jnp.max(l2, axis=1, keepdims=True)
    i2 = jnp.min(jnp.where(l2 == m2, lane, HEAD_DIM), axis=1, keepdims=True)
    e = jnp.exp(m2 - m1)
    g1 = 1.0 / (1.0 + e)
    idx_ref[...] = jnp.where(lane == 0, i1, jnp.where(lane == 1, i2, 0))
    gate_ref[...] = jnp.where(lane == 0, g1, jnp.where(lane == 1, e * g1, 0.0))
    bits = lax.bitcast_convert_type(h.astype(BF16).astype(F32), jnp.uint32)
    hp_ref[...] = (bits[:, :PACK_W] & HI_MASK) | (bits[:, PACK_W:] >> 16)


def moe_router(y, gain, router, tm_pref=1088):
    m, k = y.shape
    tm = _tile(m, tm_pref)
    wr = jnp.zeros((k, HEAD_DIM), F32).at[:, :N_EXPERTS].set(router.astype(F32))
    row = lambda w: pl.BlockSpec((tm, w), lambda i: (i, 0))
    return pl.pallas_call(
        _router_kernel,
        grid=(m // tm,),
        in_specs=[row(k), pl.BlockSpec((1, k), lambda i: (0, 0)), pl.BlockSpec((k, HEAD_DIM), lambda i: (0, 0))],
        out_specs=[row(PACK_W), row(HEAD_DIM), row(HEAD_DIM)],
        out_shape=[jax.ShapeDtypeStruct((m, PACK_W), jnp.uint32),
                   jax.ShapeDtypeStruct((m, HEAD_DIM), jnp.int32),
                   jax.ShapeDtypeStruct((m, HEAD_DIM), F32)],
        compiler_params=_params("parallel"),
    )(y, gain.reshape(1, k), wr)


def _route_plan(idx, gates, n_tiles):
    n_pairs = idx.size
    ef = idx.reshape(-1)
    onehot = (ef[:, None] == jnp.arange(N_EXPERTS, dtype=jnp.int32)[None, :]).astype(jnp.int32)
    csum = jnp.cumsum(onehot, axis=0)
    rank = jnp.take_along_axis(csum, ef[:, None], axis=1)[:, 0] - 1
    counts = csum[-1]
    padded = ((counts + MOE_TM - 1) // MOE_TM) * MOE_TM
    ends = jnp.cumsum(padded)
    pos = (ends - padded)[ef] + rank
    n_valid = (ends[-1] // MOE_TM).astype(jnp.int32).reshape(1)
    tile_start = jnp.arange(n_tiles, dtype=jnp.int32) * MOE_TM
    tile_expert = jnp.sum(tile_start[:, None] >= ends[None, :], axis=1).astype(jnp.int32)
    last_used = jnp.max(jnp.where(padded > 0, jnp.arange(N_EXPERTS, dtype=jnp.int32), 0))
    tile_expert = jnp.minimum(tile_expert, last_used)
    rows = n_tiles * MOE_TM
    src_tok = jnp.zeros((rows,), jnp.int32).at[pos].set(jnp.arange(n_pairs, dtype=jnp.int32) // 2)
    gate_sorted = jnp.zeros((rows,), F32).at[pos].set(gates.reshape(-1))
    return pos.astype(jnp.int32), src_tok, gate_sorted.reshape(rows, 1), tile_expert, n_valid


GATHER_ROWS = 256


def _gather_kernel(tok_ref, src_ref, dst_ref, sem):
    base = pl.program_id(0) * GATHER_ROWS

    def copy(r):
        return pltpu.make_async_copy(src_ref.at[pl.ds(tok_ref[base + r], 1)], dst_ref.at[pl.ds(base + r, 1)], sem)

    def start(r, c):
        copy(r).start()
        return c

    def wait(r, c):
        copy(r).wait()
        return c

    lax.fori_loop(0, GATHER_ROWS, start, 0)
    lax.fori_loop(0, GATHER_ROWS, wait, 0)


def gather_rows(src, src_tok):
    rows = src_tok.shape[0]
    return pl.pallas_call(
        _gather_kernel,
        grid_spec=pltpu.PrefetchScalarGridSpec(
            num_scalar_prefetch=1,
            grid=(rows // GATHER_ROWS,),
            in_specs=[pl.BlockSpec(memory_space=pl.ANY)],
            out_specs=pl.BlockSpec(memory_space=pl.ANY),
            scratch_shapes=[pltpu.SemaphoreType.DMA(())]),
        out_shape=jax.ShapeDtypeStruct((rows, src.shape[1]), src.dtype),
        compiler_params=_params("arbitrary"),
    )(src_tok, src)


def _unpack_bf16(words):
    hi = lax.bitcast_convert_type(words & HI_MASK, F32).astype(BF16)
    lo = lax.bitcast_convert_type(words << 16, F32).astype(BF16)
    return hi, lo


def _expert_changed(te_ref):
    i = pl.program_id(1)
    return (i == 0) | (te_ref[i] != te_ref[jnp.maximum(i - 1, 0)])


def _moe_gate_up_kernel(te_ref, nv_ref, x_ref, wg_ref, wu_ref, o_ref, wgb, wub):
    i = pl.program_id(1)

    @pl.when(_expert_changed(te_ref))
    def _():
        wgb[...] = wg_ref[0].astype(BF16)
        wub[...] = wu_ref[0].astype(BF16)

    @pl.when(i < nv_ref[0])
    def _():
        hi, lo = _unpack_bf16(x_ref[...])
        g = _dot(hi, wgb[:PACK_W], _NN) + _dot(lo, wgb[PACK_W:], _NN)
        u = _dot(hi, wub[:PACK_W], _NN) + _dot(lo, wub[PACK_W:], _NN)
        o_ref[...] = (_silu(g) * u).astype(o_ref.dtype)

    @pl.when(i >= nv_ref[0])
    def _():
        o_ref[...] = jnp.zeros_like(o_ref)


def moe_gate_up(xs, w_gu, tile_expert, n_valid, tf=512):
    rows = xs.shape[0]
    n_tiles = rows // MOE_TM
    off = D_FF // tf
    return pl.pallas_call(
        _moe_gate_up_kernel,
        grid_spec=pltpu.PrefetchScalarGridSpec(
            num_scalar_prefetch=2,
            grid=(D_FF // tf, n_tiles),
            in_specs=[pl.BlockSpec((MOE_TM, PACK_W), lambda j, i, te, nv: (i, 0)),
                      pl.BlockSpec((1, D_MODEL, tf), lambda j, i, te, nv: (te[i], 0, j)),
                      pl.BlockSpec((1, D_MODEL, tf), lambda j, i, te, nv: (te[i], 0, j + off))],
            out_specs=pl.BlockSpec((MOE_TM, tf), lambda j, i, te, nv: (i, j)),
            scratch_shapes=[pltpu.VMEM((D_MODEL, tf), BF16), pltpu.VMEM((D_MODEL, tf), BF16)]),
        out_shape=jax.ShapeDtypeStruct((rows, D_FF), BF16),
        compiler_params=_params("parallel", "arbitrary"),
    )(tile_expert, n_valid, xs, w_gu, w_gu)


def _moe_down_kernel(te_ref, nv_ref, a_ref, w_ref, gate_ref, o_ref, wb):
    i = pl.program_id(1)

    @pl.when(_expert_changed(te_ref))
    def _():
        wb[...] = w_ref[0].astype(BF16)

    @pl.when(i < nv_ref[0])
    def _():
        o_ref[...] = _dot(a_ref[...], wb[...], _NN) * gate_ref[...]

    @pl.when(i >= nv_ref[0])
    def _():
        o_ref[...] = jnp.zeros_like(o_ref)


def moe_down(act, w_down, gate_sorted, tile_expert, n_valid, tn=256):
    rows = act.shape[0]
    n_tiles = rows // MOE_TM
    return pl.pallas_call(
        _moe_down_kernel,
        grid_spec=pltpu.PrefetchScalarGridSpec(
            num_scalar_prefetch=2,
            grid=(D_MODEL // tn, n_tiles),
            in_specs=[pl.BlockSpec((MOE_TM, D_FF), lambda j, i, te, nv: (i, 0)),
                      pl.BlockSpec((1, D_FF, tn), lambda j, i, te, nv: (te[i], 0, j)),
                      pl.BlockSpec((MOE_TM, 1), lambda j, i, te, nv: (i, 0))],
            out_specs=pl.BlockSpec((MOE_TM, tn), lambda j, i, te, nv: (i, j)),
            scratch_shapes=[pltpu.VMEM((D_FF, tn), BF16)]),
        out_shape=jax.ShapeDtypeStruct((rows, D_MODEL), F32),
        compiler_params=_params("parallel", "arbitrary"),
    )(tile_expert, n_valid, act, w_down, gate_sorted)


COMBINE_TOKENS = 256


def _combine_kernel(pos_ref, y_ref, src_ref, op_ref, os_ref, buf, sem, *, prompt_steps):
    i = pl.program_id(0)
    base = i * COMBINE_TOKENS

    def copy(r, k):
        return pltpu.make_async_copy(src_ref.at[pl.ds(pos_ref[2 * (base + r) + k], 1)], buf.at[k, pl.ds(r, 1)], sem)

    def start(r, c):
        copy(r, 0).start()
        copy(r, 1).start()
        return c

    def wait(r, c):
        copy(r, 0).wait()
        copy(r, 1).wait()
        return c

    lax.fori_loop(0, COMBINE_TOKENS, start, 0)
    lax.fori_loop(0, COMBINE_TOKENS, wait, 0)
    res = y_ref[...] + buf[0] + buf[1]

    @pl.when(i < prompt_steps)
    def _():
        op_ref[...] = res

    @pl.when(i >= prompt_steps)
    def _():
        os_ref[...] = res


def moe_combine(y, o_sorted, pos, n_prompt_rows):
    m, d = y.shape
    tc = COMBINE_TOKENS
    ps = n_prompt_rows // tc
    return pl.pallas_call(
        functools.partial(_combine_kernel, prompt_steps=ps),
        grid_spec=pltpu.PrefetchScalarGridSpec(
            num_scalar_prefetch=1,
            grid=(m // tc,),
            in_specs=[pl.BlockSpec((tc, d), lambda i, pos: (i, 0)), pl.BlockSpec(memory_space=pl.ANY)],
            out_specs=[pl.BlockSpec((tc, d), lambda i, pos: (jnp.minimum(i, ps - 1), 0)),
                       pl.BlockSpec((tc, d), lambda i, pos: (jnp.maximum(i - ps, 0), 0))],
            scratch_shapes=[pltpu.VMEM((2, tc, d), F32), pltpu.SemaphoreType.DMA(())]),
        out_shape=[jax.ShapeDtypeStruct((n_prompt_rows, d), F32), jax.ShapeDtypeStruct((m - n_prompt_rows, d), F32)],
        compiler_params=_params("arbitrary"),
    )(pos, y, o_sorted)


def _gate_layout(ab, rows):
    nq = GDN_HEADS // GDN_HG
    a = ab[:, :GDN_HEADS].reshape(rows // GDN_CHUNK, GDN_CHUNK, nq, GDN_HG)
    b = ab[:, GDN_HEADS:].reshape(rows // GDN_CHUNK, GDN_CHUNK, nq, GDN_HG)
    return jnp.transpose(jnp.concatenate([a, b], axis=3), (0, 2, 3, 1))


def _memory_kv(mem_rows, mem_gain, w_mkv, k_gain):
    h = rmsnorm_rows(mem_rows, mem_gain)
    kv = matmul_ws(h, w_mkv, n_cols=2 * XQ_DIM, tn=512, tm_pref=512)
    return mem_kv_split(kv, k_gain)


def kernel(x_prompt, x_sample, state_gdn, state_gdn_conv, cache_swa_k, cache_swa_v, cache_mem_k, cache_mem_v,
           mem_prompt, rel_bias, a_norm, a_w_in, a_conv, a_A_log, a_dt_bias, a_o_norm, a_w_out,
           b_norm, b_w_in, b_q_norm, b_k_norm, b_sinks, b_w_out, c_mem_norm, c_w_mkv, c_q_norm, c_k_norm,
           f_norm, d_w_gate_up, d_w_down, m_router, m_w_gate_up, m_w_down):
    n_p, seq, d = x_prompt.shape
    n_s, dec = x_sample.shape[:2]
    t_p = n_p * seq
    t_s = n_s * dec
    t = t_p + t_s
    n_mem = mem_prompt.shape[1]
    x = jnp.concatenate([x_prompt.reshape(t_p, d), x_sample.reshape(t_s, d)], axis=0).astype(F32)
    mem_rows = mem_prompt.reshape(n_p * n_mem, d).astype(F32)

    mk0, mv0 = _memory_kv(mem_rows, c_mem_norm[0], c_w_mkv[0], c_k_norm[0])
    h = rmsnorm_rows(x, a_norm[0])
    w_in = a_w_in[0]
    o2 = GDN_CONV_DIM + GDN_QK
    o4 = o2 + 2 * GDN_HEADS
    proj = matmul_ws(h, w_in, n_cols=o2, tn=512, tm_pref=1088)
    aux_w = 5 * HEAD_DIM
    w_aux = jnp.concatenate([w_in[:, o4:], w_in[:, o2:o4], jnp.zeros((d, aux_w - XQ_DIM - 2 * GDN_HEADS), F32)], axis=1)
    proj_aux = matmul_ws(h, w_aux, n_cols=aux_w, tn=aux_w, tm_pref=1088)
    gt = _gate_layout(proj_aux[:, XQ_DIM:XQ_DIM + 2 * GDN_HEADS], t)
    prm = gdn_gate_params(a_A_log[0], a_dt_bias[0])
    o_gdn_p, s_p = gdn_prompt(proj, gt, prm, a_conv[0], a_o_norm[0], n_p, seq)
    o_gdn_s, s_s = gdn_sample(proj, t_p, gt, prm, a_conv[0], a_o_norm[0], state_gdn_conv[0], state_gdn[0])
    o_mem_p = xattn_prompt(proj_aux, 0, mk0.reshape(n_p, n_mem, XQ_DIM), mv0.reshape(n_p, n_mem, XQ_DIM),
                           c_q_norm[0], n_p, seq)
    o_mem_s = xattn_sample(proj_aux, t_p, 0, cache_mem_k[0].reshape(n_s, n_mem, XQ_DIM),
                           cache_mem_v[0].reshape(n_s, n_mem, XQ_DIM), c_q_norm[0])
    mixed = jnp.concatenate([jnp.concatenate([o_gdn_p, o_gdn_s], axis=0),
                             jnp.concatenate([o_mem_p, o_mem_s], axis=0)], axis=1)
    y = matmul_ws(mixed, a_w_out[0], n_cols=d, tn=512, tm_pref=1088, residual=x)
    h = rmsnorm_rows(y, f_norm[0])
    act = matmul_ws(h, d_w_gate_up[0], n_cols=D_FF, tn=512, tm_pref=1088, out_dtype=BF16, up_col_offset=D_FF)
    y = matmul_ws(act, d_w_down[0], n_cols=d, tn=256, tm_pref=544, residual=y)
    conv_p = proj[:t_p].reshape(n_p, seq, o2)[:, seq - (GDN_CONV - 1):, :GDN_CONV_DIM]
    conv_s = proj[t_p:].reshape(n_s, dec, o2)[:, dec - (GDN_CONV - 1):, :GDN_CONV_DIM]

    mk1, mv1 = _memory_kv(mem_rows, c_mem_norm[1], c_w_mkv[1], c_k_norm[1])
    h = rmsnorm_rows(y, b_norm[0])
    swa_in = SWA_Q_HEADS * HEAD_DIM + 2 * SWA_KV_HEADS * HEAD_DIM + XQ_DIM
    proj = matmul_ws(h, b_w_in[0], n_cols=swa_in, tn=512, tm_pref=1088)
    kv_w = SWA_KV_HEADS * HEAD_DIM
    wbuf = cache_swa_k.shape[2]
    o_swa_p, kn_p = swa_prompt(proj, rel_bias, b_q_norm[0], b_k_norm[0], b_sinks[0], n_p, seq)
    o_swa_s, k_new, v_new = swa_sample(proj, t_p, cache_swa_k[0].reshape(n_s, wbuf, kv_w),
                                       cache_swa_v[0].reshape(n_s, wbuf, kv_w), rel_bias,
                                       b_q_norm[0], b_k_norm[0], b_sinks[0])
    xq_blk = (swa_in - XQ_DIM) // XQ_DIM
    o_mem_p = xattn_prompt(proj, xq_blk, mk1.reshape(n_p, n_mem, XQ_DIM), mv1.reshape(n_p, n_mem, XQ_DIM),
                           c_q_norm[1], n_p, seq)
    o_mem_s = xattn_sample(proj, t_p, xq_blk, cache_mem_k[1].reshape(n_s, n_mem, XQ_DIM),
                           cache_mem_v[1].reshape(n_s, n_mem, XQ_DIM), c_q_norm[1])
    mixed = jnp.concatenate([jnp.concatenate([o_swa_p, o_swa_s], axis=0),
                             jnp.concatenate([o_mem_p, o_mem_s], axis=0)], axis=1)
    y = matmul_ws(mixed, b_w_out[0], n_cols=d, tn=512, tm_pref=1088, residual=y)
    v_off = SWA_Q_HEADS * HEAD_DIM + kv_w
    wl = min(WINDOW, seq)
    v_p = proj[:t_p].reshape(n_p, seq, swa_in)[:, seq - wl:, v_off:v_off + kv_w]

    hp, idx, gates = moe_router(y, f_norm[1], m_router[0])
    n_tiles = (2 * t) // MOE_TM + N_EXPERTS
    pos, src_tok, gate_sorted, tile_expert, n_valid = _route_plan(idx[:, :2], gates[:, :2], n_tiles)
    xs = gather_rows(hp, src_tok)
    act = moe_gate_up(xs, m_w_gate_up[0], tile_expert, n_valid)
    o_sorted = moe_down(act, m_w_down[0], gate_sorted, tile_expert, n_valid)
    y_p, y_s = moe_combine(y, o_sorted, pos, t_p)

    kv_shape = (SWA_KV_HEADS, HEAD_DIM)
    mem_shape = (n_p, n_mem, MEM_HEADS, HEAD_DIM)
    return (y_p.reshape(n_p, seq, d), y_s.reshape(n_s, dec, d),
            s_p[None], s_s[None], conv_p[None], conv_s[None],
            kn_p.reshape((1, n_p, wl) + kv_shape), k_new.reshape((1, n_s, wbuf) + kv_shape),
            v_p.reshape((1, n_p, wl) + kv_shape), v_new.reshape((1, n_s, wbuf) + kv_shape),
            jnp.stack([mk0.reshape(mem_shape), mk1.reshape(mem_shape)]),
            jnp.stack([mv0.reshape(mem_shape), mv1.reshape(mem_shape)]))
```

```python
import functools
import math

import numpy as np
import jax
import jax.numpy as jnp
from jax import lax
from jax.experimental import pallas as pl
from jax.experimental.pallas import tpu as pltpu

F32 = jnp.float32
BF16 = jnp.bfloat16

D_MODEL = 2048
HEAD_DIM = 128
GDN_HEADS = 12
GDN_CONV = 4
GDN_CHUNK = 64
GDN_QK = GDN_HEADS * HEAD_DIM
GDN_CONV_DIM = 3 * GDN_QK
SWA_Q_HEADS = 12
SWA_KV_HEADS = 4
SWA_GROUP = SWA_Q_HEADS // SWA_KV_HEADS
WINDOW = 128
REL_BUCKETS = 32
REL_MAX_DIST = 128
MEM_HEADS = 4
XQ_DIM = MEM_HEADS * HEAD_DIM
D_FF = 7168
N_EXPERTS = 8
NORM_EPS = 1e-6
NEG_BIG = -1e30

V7X_VMEM_LIMIT_BYTES = 56 * 1024 * 1024
HIGHEST = lax.Precision.HIGHEST


def _params(*sem):
    return pltpu.CompilerParams(dimension_semantics=sem, vmem_limit_bytes=V7X_VMEM_LIMIT_BYTES)


def _tile(n, pref):
    if n <= pref:
        return n
    for t in range(pref, 7, -1):
        if n % t == 0 and t % 8 == 0:
            return t
    return n


def _silu(x):
    return x * (1.0 / (1.0 + jnp.exp(-x)))


def _rms(x, gain):
    return x * lax.rsqrt(jnp.mean(x * x, axis=-1, keepdims=True) + NORM_EPS) * gain


def _parts(src):
    return tuple(src) if isinstance(src, (tuple, list)) else (src,)


def _n_rows(src):
    return sum(p.shape[0] for p in _parts(src))


def _row_specs(src, tm, width, col, row_arg):
    parts = _parts(src)
    if len(parts) == 1:
        return [pl.BlockSpec((tm, width), lambda *g: (g[row_arg], col(*g)))]
    ps = parts[0].shape[0] // tm
    assert parts[0].shape[0] % tm == 0 and parts[1].shape[0] % tm == 0
    return [pl.BlockSpec((tm, width), lambda *g: (jnp.minimum(g[row_arg], ps - 1), col(*g))),
            pl.BlockSpec((tm, width), lambda *g: (jnp.maximum(g[row_arg] - ps, 0), col(*g)))]


def _row_meta(src, tm):
    parts = _parts(src)
    return len(parts), parts[0].shape[0] // tm


def _read_rows(refs, i, meta):
    if meta[0] == 1:
        return refs[0][...]
    return jnp.where(i < meta[1], refs[0][...], refs[1][...])


def _rmsnorm_kernel(*refs, meta):
    *x_refs, g_ref, o_ref = refs
    x = _read_rows(x_refs, pl.program_id(0), meta)
    o_ref[...] = _rms(x, g_ref[...]).astype(o_ref.dtype)


def rmsnorm_rows(x, gain, out_dtype=BF16, tm_pref=1088):
    m = _n_rows(x)
    k = _parts(x)[0].shape[1]
    tm = _tile(m, tm_pref) if len(_parts(x)) == 1 else 512
    return pl.pallas_call(
        functools.partial(_rmsnorm_kernel, meta=_row_meta(x, tm)),
        grid=(m // tm,),
        in_specs=_row_specs(x, tm, k, lambda i: 0, 0) + [pl.BlockSpec((1, k), lambda i: (0, 0))],
        out_specs=pl.BlockSpec((tm, k), lambda i: (i, 0)),
        out_shape=jax.ShapeDtypeStruct((m, k), out_dtype),
        compiler_params=_params("parallel"),
    )(*_parts(x), gain.reshape(1, k))


def _mm_kernel(*refs, lhs, residual, swiglu):
    refs = list(refs)
    i = pl.program_id(1)
    a_vals = []
    for meta in lhs:
        a_vals.append(_read_rows(refs[:meta[0]], i, meta))
        del refs[:meta[0]]
    w_ref = refs.pop(0)
    w2_ref = refs.pop(0) if swiglu else None
    res = None
    if residual is not None:
        res = _read_rows(refs[:residual[0]], i, residual)
        del refs[:residual[0]]
    o_ref, wb_ref = refs[0], refs[1]

    @pl.when(i == 0)
    def _():
        wb_ref[...] = w_ref[...].astype(BF16)
        if swiglu:
            refs[2][...] = w2_ref[...].astype(BF16)

    def product(wb):
        acc, k0 = None, 0
        for a in a_vals:
            part = jnp.dot(a, wb[k0:k0 + a.shape[1]], preferred_element_type=F32)
            acc = part if acc is None else acc + part
            k0 += a.shape[1]
        return acc

    acc = product(wb_ref)
    if swiglu:
        acc = _silu(acc) * product(refs[2])
    if res is not None:
        acc = acc + res
    o_ref[...] = acc.astype(o_ref.dtype)


def matmul_ws(lhs, w, *, n_cols, tn, tm_pref, layer=0, out_dtype=F32, residual=None, up_col_offset=None,
              weight_buffers=2):
    lhs = list(lhs) if isinstance(lhs, list) else [lhs]
    m = _n_rows(lhs[0])
    k = sum(_parts(seg)[0].shape[1] for seg in lhs)
    assert w.shape[-2] == k and n_cols % tn == 0
    paired = any(len(_parts(s)) > 1 for s in lhs + ([residual] if residual is not None else []))
    tm = 512 if paired else _tile(m, tm_pref)
    swiglu = up_col_offset is not None
    assert w.ndim == 3
    wmode = {} if weight_buffers == 2 else {"pipeline_mode": pl.Buffered(weight_buffers)}

    def wspec(off):
        return pl.BlockSpec((None, k, tn), lambda j, i: (layer, 0, j + off), **wmode)

    in_specs, args = [], []
    for seg in lhs:
        in_specs += _row_specs(seg, tm, _parts(seg)[0].shape[1], lambda j, i: 0, 1)
        args += _parts(seg)
    in_specs.append(wspec(0))
    args.append(w)
    scratch = [pltpu.VMEM((k, tn), BF16)]
    if swiglu:
        assert up_col_offset % tn == 0
        in_specs.append(wspec(up_col_offset // tn))
        args.append(w)
        scratch.append(pltpu.VMEM((k, tn), BF16))
    if residual is not None:
        in_specs += _row_specs(residual, tm, tn, lambda j, i: j, 1)
        args += _parts(residual)
    return pl.pallas_call(
        functools.partial(_mm_kernel, lhs=[_row_meta(s, tm) for s in lhs],
                          residual=None if residual is None else _row_meta(residual, tm), swiglu=swiglu),
        grid=(n_cols // tn, m // tm),
        in_specs=in_specs,
        out_specs=pl.BlockSpec((tm, tn), lambda j, i: (i, j)),
        out_shape=jax.ShapeDtypeStruct((m, n_cols), out_dtype),
        scratch_shapes=scratch,
        compiler_params=_params("parallel", "arbitrary"),
    )(*args)


def _dot(a, b, dims, precision=None):
    return lax.dot_general(a, b, (dims, ((), ())), precision=precision, preferred_element_type=F32)


_NN = ((1,), (0,))
_NT = ((1,), (1,))
_TN = ((0,), (0,))


def _bdot(a, b, dims):
    return _dot(a.astype(BF16), b.astype(BF16), dims)


def _conv_silu(u, hist, w, row_in_seq=None):
    rows, c = u.shape
    y = u * w[GDN_CONV - 1:GDN_CONV]
    for s in range(1, GDN_CONV):
        r = pltpu.roll(u, s, axis=0)
        if row_in_seq is None:
            rows8 = lax.broadcasted_iota(jnp.int32, (8, c), 0)
            head = jnp.where(rows8 < s, pltpu.roll(hist, s, axis=0), r[0:8])
            r = jnp.concatenate([head, r[8:]], axis=0)
        else:
            r = jnp.where(row_in_seq >= s, r, pltpu.roll(hist, rows - 4 + s, axis=0))
        y = y + r * w[GDN_CONV - 1 - s:GDN_CONV - s]
    return _silu(y)


def _l2n(x):
    return x * lax.rsqrt(jnp.sum(x * x, axis=-1, keepdims=True) + NORM_EPS)


def _gates(x, prm):
    a_log = prm[:, 0:1]
    dt = prm[:, 1:2]
    xa = x + dt
    softplus = jnp.maximum(xa, 0.0) + jnp.log(1.0 + jnp.exp(-jnp.abs(xa)))
    g = -jnp.exp(a_log) * softplus
    beta = 1.0 / (1.0 + jnp.exp(-x))
    return g[0:4], beta[4:8]


def _split(x):
    hi = x.astype(BF16)
    return hi, (x - hi.astype(F32)).astype(BF16)


def _mm3(lhs, rhs):
    n = len(lhs)
    r = lhs[0].shape[0]
    rh, rl = _split(rhs)
    parts = [_split(x) for x in lhs]
    his = [p[0] for p in parts]
    a = _dot(jnp.concatenate(his + [p[1] for p in parts], axis=0), rh, _NN)
    b = _dot(jnp.concatenate(his, axis=0) if n > 1 else his[0], rl, _NN)
    return [a[i * r:(i + 1) * r] + a[(n + i) * r:(n + i + 1) * r] + b[i * r:(i + 1) * r] for i in range(n)]


def _chunk_prep(qs, ks, vs, g_rows, b_rows, blk):
    c = GDN_CHUNK
    n = len(qs)
    lg = int(math.log2(blk))
    ii = lax.broadcasted_iota(jnp.int32, (c, c), 0)
    jj = lax.broadcasted_iota(jnp.int32, (c, c), 1)
    same = (ii >> lg) == (jj >> lg)
    lower = same & (ii >= jj)
    strict = same & (ii > jj)
    eye = ii == jj
    last = jj == (((ii >> lg) << lg) + (blk - 1))
    csum = jnp.where(same & (ii <= jj), 1.0, 0.0).astype(BF16)
    g = jnp.concatenate(g_rows, axis=0)
    g1 = g.astype(BF16)
    r1 = g - g1.astype(F32)
    g2 = r1.astype(BF16)
    g3 = (r1 - g2.astype(F32)).astype(BF16)
    pad = (-3 * n) % 8
    pieces = [g1, g2, g3] + ([jnp.zeros((pad, c), BF16)] if pad else [])
    gc3 = _dot(jnp.concatenate(pieces, axis=0), csum, _NN)
    gc_rows = gc3[0:n] + gc3[n:2 * n] + gc3[2 * n:3 * n]

    def col(mask, row):
        return jnp.sum(jnp.where(mask, jnp.broadcast_to(row, (c, c)), 0.0), axis=1, keepdims=True)

    gcb = [jnp.broadcast_to(gc_rows[i:i + 1], (c, c)) for i in range(n)]
    gc_col = [col(eye, gc_rows[i:i + 1]) for i in range(n)]
    gl_col = [col(last, gc_rows[i:i + 1]) for i in range(n)]
    beta_col = [col(eye, b_rows[i]) for i in range(n)]
    decay = [jnp.where(lower, jnp.exp(jnp.where(lower, gc_col[i] - gcb[i], 0.0)), 0.0) for i in range(n)]
    kb = [ks[i] * beta_col[i] for i in range(n)]
    kq = [_bdot(jnp.concatenate([kb[i], qs[i]], axis=0), ks[i], _NT) for i in range(n)]
    a = [jnp.where(strict, kq[i][:c] * decay[i], 0.0) for i in range(n)]
    p = [kq[i][c:] * decay[i] for i in range(n)]
    ident = jnp.where(eye, 1.0, 0.0)
    t = [ident - a[i] for i in range(n)]
    pw = [_mm3([a[i]], a[i])[0] for i in range(n)]
    for _ in range(lg - 2):
        nxt = [_mm3([pw[i], t[i]], pw[i]) for i in range(n)]
        t = [t[i] + nxt[i][1] for i in range(n)]
        pw = [nxt[i][0] for i in range(n)]
    t = [t[i] + _mm3([t[i]], pw[i])[0] for i in range(n)]
    eg = [jnp.exp(gc_col[i]) for i in range(n)]
    uw = [_mm3([t[i]], jnp.concatenate([vs[i] * beta_col[i], kb[i] * eg[i]], axis=1))[0] for i in range(n)]
    return [(uw[i][:, :HEAD_DIM], uw[i][:, HEAD_DIM:], p[i], qs[i] * eg[i],
             ks[i] * jnp.exp(gl_col[i] - gc_col[i]), jnp.exp(gl_col[i])) for i in range(n)]


def _gdn_out(o, z, o_gain):
    return (_rms(o, o_gain) * _silu(z)).astype(BF16)


GDN_HG = 4
GDN_HW = GDN_HG * HEAD_DIM


def _gdn_prompt_kernel(q_ref, k_ref, v_ref, z_ref, gt_ref, prm_ref, wq_ref, wk_ref, wv_ref, og_ref,
                       o_ref, s_ref, state, hist, *, n_chunks):
    @pl.when(pl.program_id(2) == 0)
    def _():
        state[...] = jnp.zeros_like(state)
        hist[...] = jnp.zeros_like(hist)

    qkv = []
    for idx, (u_ref, w_ref) in enumerate(((q_ref, wq_ref), (k_ref, wk_ref), (v_ref, wv_ref))):
        u = u_ref[...]
        qkv.append(_conv_silu(u, hist[idx], w_ref[...]))
        hist[idx] = u[u.shape[0] - 8:]
    qc, kc, vc = qkv
    og = og_ref[...]
    keys, qs, ks, vs, g_rows, b_rows = [], [], [], [], [], []
    for c in range(n_chunks):
        rows = slice(c * GDN_CHUNK, (c + 1) * GDN_CHUNK)
        g4, b4 = _gates(gt_ref[c, 0], prm_ref[0])
        for hh in range(GDN_HG):
            cols = slice(hh * HEAD_DIM, (hh + 1) * HEAD_DIM)
            keys.append((c, hh))
            qs.append(_l2n(qc[rows, cols]) * (HEAD_DIM ** -0.5))
            ks.append(_l2n(kc[rows, cols]))
            vs.append(vc[rows, cols])
            g_rows.append(g4[hh:hh + 1])
            b_rows.append(b4[hh:hh + 1])
    chunks = _chunk_prep(qs, ks, vs, g_rows, b_rows, GDN_CHUNK)
    wu = [jnp.concatenate([w, u], axis=1) for (u, w, _, _, _, _) in chunks]
    kwu = [_bdot(ch[4], wu_i, _TN) for ch, wu_i in zip(chunks, wu)]
    pwu = [_bdot(ch[2], wu_i, _NN) for ch, wu_i in zip(chunks, wu)]
    prep = {}
    for key, ch, kwu_i, pwu_i in zip(keys, chunks, kwu, pwu):
        prep[key] = (ch[3] - pwu_i[:, :HEAD_DIM], pwu_i[:, HEAD_DIM:], kwu_i[:, :HEAD_DIM], kwu_i[:, HEAD_DIM:],
                     ch[5][0:1, 0:1])
    s = [state[hh] for hh in range(GDN_HG)]
    for c in range(n_chunks):
        rows = slice(c * GDN_CHUNK, (c + 1) * GDN_CHUNK)
        for hh in range(GDN_HG):
            cols = slice(hh * HEAD_DIM, (hh + 1) * HEAD_DIM)
            qp, op, mneg, bmat, gl = prep[c, hh]
            sb = s[hh].astype(BF16)
            r = _dot(jnp.concatenate([mneg, qp], axis=0).astype(BF16), sb, _NN)
            s[hh] = gl * s[hh] - r[:HEAD_DIM] + bmat
            o_ref[rows, cols] = _gdn_out(r[HEAD_DIM:] + op, z_ref[rows, cols], og)
    for hh in range(GDN_HG):
        state[hh] = s[hh]
        s_ref[0, hh] = s[hh]


def gdn_prompt(proj, gt, prm, conv_w, o_gain, n_seq, seq_len, lb=256):
    nlb = seq_len // lb
    n_chunks = lb // GDN_CHUNK
    nq = GDN_QK // GDN_HW

    def rowcol(seg):
        return pl.BlockSpec((lb, GDN_HW), lambda n, g, l: (n * nlb + l, seg * nq + g))

    def wspec(seg):
        return pl.BlockSpec((GDN_CONV, GDN_HW), lambda n, g, l: (0, seg * nq + g))

    return pl.pallas_call(
        functools.partial(_gdn_prompt_kernel, n_chunks=n_chunks),
        grid=(n_seq, nq, nlb),
        in_specs=[rowcol(0), rowcol(1), rowcol(2), rowcol(3),
                  pl.BlockSpec((n_chunks, 1, 8, GDN_CHUNK), lambda n, g, l: (n * nlb + l, g, 0, 0)),
                  pl.BlockSpec((1, 8, 2), lambda n, g, l: (g, 0, 0)),
                  wspec(0), wspec(1), wspec(2),
                  pl.BlockSpec((1, HEAD_DIM), lambda n, g, l: (0, 0))],
        out_specs=[pl.BlockSpec((lb, GDN_HW), lambda n, g, l: (n * nlb + l, g)),
                   pl.BlockSpec((1, GDN_HG, HEAD_DIM, HEAD_DIM), lambda n, g, l: (n, g, 0, 0))],
        out_shape=[jax.ShapeDtypeStruct((n_seq * seq_len, GDN_QK), BF16),
                   jax.ShapeDtypeStruct((n_seq, GDN_HEADS, HEAD_DIM, HEAD_DIM), F32)],
        scratch_shapes=[pltpu.VMEM((GDN_HG, HEAD_DIM, HEAD_DIM), F32), pltpu.VMEM((3, 8, GDN_HW), F32)],
        compiler_params=_params("parallel", "parallel", "arbitrary"),
    )(proj, proj, proj, proj, gt, prm, conv_w, conv_w, conv_w, o_gain.reshape(1, HEAD_DIM))


def gdn_gate_params(a_log, dt_bias):
    p = jnp.stack([a_log.astype(F32), dt_bias.astype(F32)], axis=-1).reshape(GDN_HEADS // GDN_HG, GDN_HG, 2)
    return jnp.concatenate([p, jnp.zeros_like(p)], axis=1)


GDN_SEQ_PER_STEP = GDN_CHUNK // 4


def _gdn_sample_kernel(q_ref, k_ref, v_ref, z_ref, gt_ref, prm_ref, wq_ref, wk_ref, wv_ref, og_ref,
                       hq_ref, hk_ref, hv_ref, s0_ref, o_ref, s_ref):
    rows = GDN_CHUNK
    row_in_seq = lax.broadcasted_iota(jnp.int32, (rows, GDN_HW), 0) & 3
    qc = _conv_silu(q_ref[...], hq_ref[...], wq_ref[...], row_in_seq)
    kc = _conv_silu(k_ref[...], hk_ref[...], wk_ref[...], row_in_seq)
    vc = _conv_silu(v_ref[...], hv_ref[...], wv_ref[...], row_in_seq)
    g4, b4 = _gates(gt_ref[0, 0], prm_ref[0])
    og = og_ref[...]
    seq_of_row = lax.broadcasted_iota(jnp.int32, (rows, 1), 0) >> 2
    head_cols = [slice(hh * HEAD_DIM, (hh + 1) * HEAD_DIM) for hh in range(GDN_HG)]
    chunks = _chunk_prep([_l2n(qc[:, cols]) * (HEAD_DIM ** -0.5) for cols in head_cols],
                         [_l2n(kc[:, cols]) for cols in head_cols],
                         [vc[:, cols] for cols in head_cols],
                         [g4[hh:hh + 1] for hh in range(GDN_HG)],
                         [b4[hh:hh + 1] for hh in range(GDN_HG)], 4)
    for hh in range(GDN_HG):
        cols = head_cols[hh]
        u, w, p, qg, kg, gl = chunks[hh]
        lhs = jnp.concatenate([w, qg], axis=0).astype(BF16)
        ws = jnp.zeros((rows, HEAD_DIM), F32)
        qs = jnp.zeros((rows, HEAD_DIM), F32)
        for b in range(GDN_SEQ_PER_STEP):
            r = _dot(lhs, s0_ref[b, hh].astype(BF16), _NN)
            sel = seq_of_row == b
            ws = jnp.where(sel, r[:rows], ws)
            qs = jnp.where(sel, r[rows:], qs)
        v_new = u - ws
        o = qs + _bdot(p, v_new, _NN)
        o_ref[:, cols] = _gdn_out(o, z_ref[:, cols], og)
        v_new_b = v_new.astype(BF16)
        for b in range(GDN_SEQ_PER_STEP):
            kgm = jnp.where(seq_of_row == b, kg, 0.0).astype(BF16)
            s_ref[b, hh] = s0_ref[b, hh] * gl[4 * b:4 * b + 1, 0:1] + _dot(kgm, v_new_b, _TN)


def gdn_sample(proj, row_off, gt, prm, conv_w, o_gain, conv_buf, s0):
    nb = s0.shape[0]
    assert nb % GDN_SEQ_PER_STEP == 0 and row_off % GDN_CHUNK == 0
    rb = row_off // GDN_CHUNK
    nq = GDN_QK // GDN_HW
    hist = jnp.pad(conv_buf.astype(F32), ((0, 0), (1, 0), (0, 0))).reshape(nb * 4, GDN_CONV_DIM)

    def rowcol(seg):
        return pl.BlockSpec((GDN_CHUNK, GDN_HW), lambda b, g: (rb + b, seg * nq + g))

    def hspec(seg):
        return pl.BlockSpec((GDN_CHUNK, GDN_HW), lambda b, g: (b, seg * nq + g))

    def wspec(seg):
        return pl.BlockSpec((GDN_CONV, GDN_HW), lambda b, g: (0, seg * nq + g))

    sspec = pl.BlockSpec((GDN_SEQ_PER_STEP, GDN_HG, HEAD_DIM, HEAD_DIM), lambda b, g: (b, g, 0, 0))
    return pl.pallas_call(
        _gdn_sample_kernel,
        grid=(nb // GDN_SEQ_PER_STEP, nq),
        in_specs=[rowcol(0), rowcol(1), rowcol(2), rowcol(3),
                  pl.BlockSpec((1, 1, 8, GDN_CHUNK), lambda b, g: (rb + b, g, 0, 0)),
                  pl.BlockSpec((1, 8, 2), lambda b, g: (g, 0, 0)),
                  wspec(0), wspec(1), wspec(2),
                  pl.BlockSpec((1, HEAD_DIM), lambda b, g: (0, 0)),
                  hspec(0), hspec(1), hspec(2), sspec],
        out_specs=[pl.BlockSpec((GDN_CHUNK, GDN_HW), lambda b, g: (b, g)), sspec],
        out_shape=[jax.ShapeDtypeStruct((nb * 4, GDN_QK), BF16),
                   jax.ShapeDtypeStruct(s0.shape, F32)],
        compiler_params=_params("parallel", "parallel"),
    )(proj, proj, proj, proj, gt, prm, conv_w, conv_w, conv_w, o_gain.reshape(1, HEAD_DIM),
      hist, hist, hist, s0.astype(F32))


def _softmax_pv(s, v):
    m = jnp.max(s, axis=-1, keepdims=True)
    p = jnp.exp(s - m)
    den = jnp.sum(p, axis=-1, keepdims=True)
    return _bdot(p, v, _NN) / den


def _xattn_prompt_kernel(q_ref, k_ref, v_ref, g_ref, o_ref):
    gain = g_ref[...]
    for h in range(MEM_HEADS):
        cols = slice(h * HEAD_DIM, (h + 1) * HEAD_DIM)
        q = _rms(q_ref[:, cols], gain)
        s = _bdot(q, k_ref[0, :, cols], _NT) * (HEAD_DIM ** -0.5)
        o_ref[:, cols] = _softmax_pv(s, v_ref[0, :, cols]).astype(o_ref.dtype)


def xattn_prompt(qsrc, col_blk, mk, mv, q_gain, n_seq, seq_len, tq=512):
    nq = seq_len // tq
    m = mk.shape[1]
    return pl.pallas_call(
        _xattn_prompt_kernel,
        grid=(n_seq, nq),
        in_specs=[pl.BlockSpec((tq, XQ_DIM), lambda n, i: (n * nq + i, col_blk)),
                  pl.BlockSpec((1, m, XQ_DIM), lambda n, i: (n, 0, 0)),
                  pl.BlockSpec((1, m, XQ_DIM), lambda n, i: (n, 0, 0)),
                  pl.BlockSpec((1, HEAD_DIM), lambda n, i: (0, 0))],
        out_specs=pl.BlockSpec((tq, XQ_DIM), lambda n, i: (n * nq + i, 0)),
        out_shape=jax.ShapeDtypeStruct((n_seq * seq_len, XQ_DIM), BF16),
        compiler_params=_params("parallel", "parallel"),
    )(qsrc, mk, mv, q_gain.reshape(1, HEAD_DIM))


SEQ_PER_STEP = 8
ROWS_PER_STEP = 4 * SEQ_PER_STEP


def _head_rows(ref, seq, head, n_tok, n_heads):
    return ref[pl.ds(seq * n_tok * n_heads + head, n_tok, stride=n_heads), :]


def _xattn_sample_kernel(q_ref, k_ref, v_ref, g_ref, o_ref, *, n_mem):
    gain = g_ref[...]
    seq_of_row = lax.broadcasted_iota(jnp.int32, (ROWS_PER_STEP, 1), 0) >> 2
    for h in range(MEM_HEADS):
        cols = slice(h * HEAD_DIM, (h + 1) * HEAD_DIM)
        q = _rms(q_ref[:, cols], gain).astype(BF16)
        s = jnp.zeros((ROWS_PER_STEP, n_mem), F32)
        for b in range(SEQ_PER_STEP):
            k = _head_rows(k_ref, b, h, n_mem, MEM_HEADS).astype(BF16)
            s = jnp.where(seq_of_row == b, _dot(q, k, _NT), s)
        s = s * (HEAD_DIM ** -0.5)
        p = jnp.exp(s - jnp.max(s, axis=-1, keepdims=True))
        den = jnp.sum(p, axis=-1, keepdims=True)
        pb = p.astype(BF16)
        o = jnp.zeros((ROWS_PER_STEP, HEAD_DIM), F32)
        for b in range(SEQ_PER_STEP):
            v = _head_rows(v_ref, b, h, n_mem, MEM_HEADS).astype(BF16)
            o = jnp.where(seq_of_row == b, _dot(pb, v, _NN), o)
        o_ref[:, cols] = (o / den).astype(o_ref.dtype)


def xattn_sample(qsrc, row_off, col_blk, mk, mv, layer, q_gain):
    n_layers, nb, m = mk.shape[:3]
    rb = row_off // ROWS_PER_STEP
    steps = nb // SEQ_PER_STEP
    blk_rows = SEQ_PER_STEP * m * MEM_HEADS
    kv = pl.BlockSpec((blk_rows, HEAD_DIM), lambda b: (layer * steps + b, 0))
    flat = (n_layers * nb * m * MEM_HEADS, HEAD_DIM)
    return pl.pallas_call(
        functools.partial(_xattn_sample_kernel, n_mem=m),
        grid=(steps,),
        in_specs=[pl.BlockSpec((ROWS_PER_STEP, XQ_DIM), lambda b: (rb + b, col_blk)), kv, kv,
                  pl.BlockSpec((1, HEAD_DIM), lambda b: (0, 0))],
        out_specs=pl.BlockSpec((ROWS_PER_STEP, XQ_DIM), lambda b: (b, 0)),
        out_shape=jax.ShapeDtypeStruct((nb * 4, XQ_DIM), BF16),
        compiler_params=_params("parallel"),
    )(qsrc, mk.reshape(flat), mv.reshape(flat), q_gain.reshape(1, HEAD_DIM))


def _mem_kv_split_kernel(kv_ref, g_ref, k_ref, v_ref):
    gain = g_ref[...]
    for h in range(MEM_HEADS):
        cols = slice(h * HEAD_DIM, (h + 1) * HEAD_DIM)
        k_ref[:, cols] = _rms(kv_ref[:, cols], gain)
    v_ref[...] = kv_ref[:, XQ_DIM:]


def mem_kv_split(kv, k_gain):
    rows = kv.shape[0]
    return pl.pallas_call(
        _mem_kv_split_kernel,
        grid=(1,),
        in_specs=[pl.BlockSpec((rows, 2 * XQ_DIM), lambda i: (0, 0)), pl.BlockSpec((1, HEAD_DIM), lambda i: (0, 0))],
        out_specs=[pl.BlockSpec((rows, XQ_DIM), lambda i: (0, 0))] * 2,
        out_shape=[jax.ShapeDtypeStruct((rows, XQ_DIM), F32)] * 2,
        compiler_params=_params("arbitrary"),
    )(kv, k_gain.reshape(1, HEAD_DIM))


def _t5_bucket(dist):
    n = np.maximum(dist, 0)
    max_exact = REL_BUCKETS // 2
    large = max_exact + (np.log(np.maximum(n, 1) / max_exact) / np.log(REL_MAX_DIST / max_exact)
                         * (REL_BUCKETS - max_exact)).astype(np.int64)
    large = np.minimum(large, REL_BUCKETS - 1)
    return np.where(n < max_exact, n, large).astype(np.int32)


def _bias_table(rel_bias, dist, valid):
    b = jnp.transpose(rel_bias.astype(F32)[_t5_bucket(dist)], (2, 0, 1))
    return jnp.where(jnp.asarray(valid)[None], b, NEG_BIG)


def _sink_softmax_pv(s, sink, v_parts):
    m = sink
    for si in s:
        m = jnp.maximum(m, jnp.max(si, axis=-1, keepdims=True))
    den = jnp.exp(sink - m)
    o = None
    for si, vi in zip(s, v_parts):
        p = jnp.exp(si - m)
        den = den + jnp.sum(p, axis=-1, keepdims=True)
        pv = vi(p.astype(BF16))
        o = pv if o is None else o + pv
    return o / den


def _swa_prompt_kernel(q_ref, kc_ref, kp_ref, vc_ref, vp_ref, bucket_ref, rb_ref, qg_ref, kg_ref, sink_ref,
                       o_ref, kn_ref, bias_ref):
    first = pl.program_id(1) == 0

    @pl.when(first)
    def _():
        bucket = bucket_ref[...]
        for h in range(SWA_Q_HEADS):
            acc = jnp.full(bucket.shape, NEG_BIG, F32)
            for b in range(REL_BUCKETS):
                acc = jnp.where(bucket == b, rb_ref[b, h], acc)
            bias_ref[h] = acc

    qg = qg_ref[...]
    kg = kg_ref[...]
    blk = q_ref.shape[0]
    prev_col = lax.broadcasted_iota(jnp.int32, (SWA_GROUP * blk, 2 * blk), 1) < blk
    for g in range(SWA_KV_HEADS):
        kcols = slice(g * HEAD_DIM, (g + 1) * HEAD_DIM)
        k_cur = _rms(kc_ref[:, kcols], kg)
        kn_ref[0, :, kcols] = k_cur
        k_all = jnp.concatenate([_rms(kp_ref[:, kcols], kg), k_cur], axis=0)
        v_all = jnp.concatenate([vp_ref[:, kcols], vc_ref[:, kcols]], axis=0).astype(BF16)
        heads = range(g * SWA_GROUP, (g + 1) * SWA_GROUP)
        q = jnp.concatenate([_rms(q_ref[:, h * HEAD_DIM:(h + 1) * HEAD_DIM], qg) for h in heads], axis=0)
        s = _bdot(q, k_all, _NT) * (HEAD_DIM ** -0.5)
        s = s + jnp.concatenate([bias_ref[h] for h in heads], axis=0)
        s = jnp.where(prev_col & first, NEG_BIG, s)
        sink = jnp.concatenate([jnp.broadcast_to(sink_ref[0:1, h:h + 1], (blk, 1)) for h in heads], axis=0)
        o = _sink_softmax_pv([s], sink, [lambda p: _dot(p, v_all, _NN)])
        for i, h in enumerate(heads):
            o_ref[:, h * HEAD_DIM:(h + 1) * HEAD_DIM] = o[i * blk:(i + 1) * blk].astype(o_ref.dtype)


def swa_prompt(proj, rel_bias, q_gain, k_gain, sinks, n_seq, seq_len):
    blk = WINDOW
    nb = seq_len // blk
    qi = np.arange(blk)[:, None]
    kj = np.arange(2 * blk)[None, :]
    dist = blk + qi - kj
    bucket = jnp.asarray(np.where((dist >= 0) & (dist < WINDOW), _t5_bucket(dist), -1).astype(np.int32))
    kv_w = SWA_KV_HEADS * HEAD_DIM
    kblk = (SWA_Q_HEADS * HEAD_DIM) // kv_w

    def cur(seg):
        return pl.BlockSpec((blk, kv_w), lambda n, i: (n * nb + i, kblk + seg))

    def prev(seg):
        return pl.BlockSpec((blk, kv_w), lambda n, i: (n * nb + jnp.maximum(i - 1, 0), kblk + seg))

    vec = pl.BlockSpec((1, HEAD_DIM), lambda n, i: (0, 0))
    sink_row = jnp.zeros((1, HEAD_DIM), F32).at[0, :SWA_Q_HEADS].set(sinks.astype(F32))
    return pl.pallas_call(
        _swa_prompt_kernel,
        grid=(n_seq, nb),
        in_specs=[pl.BlockSpec((blk, SWA_Q_HEADS * HEAD_DIM), lambda n, i: (n * nb + i, 0)),
                  cur(0), prev(0), cur(1), prev(1),
                  pl.BlockSpec((blk, 2 * blk), lambda n, i: (0, 0)),
                  pl.BlockSpec(memory_space=pltpu.SMEM),
                  vec, vec, vec],
        out_specs=[pl.BlockSpec((blk, SWA_Q_HEADS * HEAD_DIM), lambda n, i: (n * nb + i, 0)),
                   pl.BlockSpec((1, blk, kv_w), lambda n, i: (n, 0, 0))],
        out_shape=[jax.ShapeDtypeStruct((n_seq * seq_len, SWA_Q_HEADS * HEAD_DIM), BF16),
                   jax.ShapeDtypeStruct((n_seq, blk, kv_w), F32)],
        scratch_shapes=[pltpu.VMEM((SWA_Q_HEADS, blk, 2 * blk), F32)],
        compiler_params=_params("parallel", "arbitrary"),
    )(proj, proj, proj, proj, proj, bucket, rel_bias.astype(F32),
      q_gain.reshape(1, HEAD_DIM), k_gain.reshape(1, HEAD_DIM), sink_row)


def _swa_sample_kernel(q_ref, kn_ref, vn_ref, kc_ref, vc_ref, bo_ref, bn_ref, qg_ref, kg_ref, sink_ref,
                       o_ref, ko_ref, vo_ref, *, w):
    qg = qg_ref[...]
    kg = kg_ref[...]
    rows = ROWS_PER_STEP
    nh = SWA_KV_HEADS
    seq_of_row = (lax.broadcasted_iota(jnp.int32, (SWA_GROUP * rows, 1), 0) & (rows - 1)) >> 2
    k_new = [_rms(kn_ref[:, g * HEAD_DIM:(g + 1) * HEAD_DIM], kg) for g in range(nh)]
    v_new = [vn_ref[:, g * HEAD_DIM:(g + 1) * HEAD_DIM] for g in range(nh)]
    kept = (w - 4) * nh
    for b in range(SEQ_PER_STEP):
        base = b * w * nh
        for src, dst, new in ((kc_ref, ko_ref, k_new), (vc_ref, vo_ref, v_new)):
            dst[pl.ds(base, kept), :] = src[pl.ds(base + 4 * nh, kept), :]
            for g in range(nh):
                dst[pl.ds(base + kept + g, 4, stride=nh), :] = new[g][4 * b:4 * b + 4]
    for g in range(nh):
        heads = range(g * SWA_GROUP, (g + 1) * SWA_GROUP)
        q = jnp.concatenate([_rms(q_ref[:, h * HEAD_DIM:(h + 1) * HEAD_DIM], qg) for h in heads],
                            axis=0).astype(BF16)
        s_old = jnp.zeros((SWA_GROUP * rows, w), F32)
        for b in range(SEQ_PER_STEP):
            s_old = jnp.where(seq_of_row == b, _dot(q, _head_rows(kc_ref, b, g, w, nh).astype(BF16), _NT), s_old)
        s_old = s_old * (HEAD_DIM ** -0.5) + bo_ref[g]
        s_new = _dot(q, k_new[g].astype(BF16), _NT) * (HEAD_DIM ** -0.5) + bn_ref[g]
        sink = jnp.concatenate([jnp.broadcast_to(sink_ref[0:1, h:h + 1], (rows, 1)) for h in heads], axis=0)

        def pv_old(p, g=g):
            o = jnp.zeros((SWA_GROUP * rows, HEAD_DIM), F32)
            for b in range(SEQ_PER_STEP):
                o = jnp.where(seq_of_row == b, _dot(p, _head_rows(vc_ref, b, g, w, nh).astype(BF16), _NN), o)
            return o

        def pv_new(p, g=g):
            return _dot(p, v_new[g].astype(BF16), _NN)

        o = _sink_softmax_pv([s_old, s_new], sink, [pv_old, pv_new])
        for i, h in enumerate(heads):
            o_ref[:, h * HEAD_DIM:(h + 1) * HEAD_DIM] = o[i * rows:(i + 1) * rows].astype(o_ref.dtype)


def swa_sample(proj, row_off, k_cache, v_cache, rel_bias, q_gain, k_gain, sinks):
    nb, w = k_cache.shape[:2]
    kv_w = SWA_KV_HEADS * HEAD_DIM
    rows = ROWS_PER_STEP
    rb = row_off // rows
    kblk = (SWA_Q_HEADS * HEAD_DIM) // kv_w
    qi = np.arange(4)[:, None]
    dist_old = w + qi - np.arange(w)[None, :]
    b_old = _bias_table(rel_bias, dist_old, (dist_old >= 0) & (dist_old < WINDOW))
    dist_new = qi - np.arange(4)[None, :]
    b_new = _bias_table(rel_bias, dist_new, dist_new >= 0)
    b_old = jnp.broadcast_to(b_old.reshape(SWA_KV_HEADS, SWA_GROUP, 1, 4, w),
                             (SWA_KV_HEADS, SWA_GROUP, SEQ_PER_STEP, 4, w)).reshape(SWA_KV_HEADS, SWA_GROUP * rows, w)
    same_seq = np.eye(SEQ_PER_STEP, dtype=bool)[None, None, :, None, :, None]
    b_new = jnp.where(same_seq, b_new.reshape(SWA_KV_HEADS, SWA_GROUP, 1, 4, 1, 4), NEG_BIG)
    b_new = b_new.reshape(SWA_KV_HEADS, SWA_GROUP * rows, rows)
    vec = pl.BlockSpec((1, HEAD_DIM), lambda b: (0, 0))
    sink_row = jnp.zeros((1, HEAD_DIM), F32).at[0, :SWA_Q_HEADS].set(sinks.astype(F32))
    cache = pl.BlockSpec((SEQ_PER_STEP * w * SWA_KV_HEADS, HEAD_DIM), lambda b: (b, 0))
    flat = (nb * w * SWA_KV_HEADS, HEAD_DIM)
    return pl.pallas_call(
        functools.partial(_swa_sample_kernel, w=w),
        grid=(nb // SEQ_PER_STEP,),
        in_specs=[pl.BlockSpec((rows, SWA_Q_HEADS * HEAD_DIM), lambda b: (rb + b, 0)),
                  pl.BlockSpec((rows, kv_w), lambda b: (rb + b, kblk)),
                  pl.BlockSpec((rows, kv_w), lambda b: (rb + b, kblk + 1)),
                  cache, cache,
                  pl.BlockSpec(b_old.shape, lambda b: (0, 0, 0)),
                  pl.BlockSpec(b_new.shape, lambda b: (0, 0, 0)),
                  vec, vec, vec],
        out_specs=[pl.BlockSpec((rows, SWA_Q_HEADS * HEAD_DIM), lambda b: (b, 0)), cache, cache],
        out_shape=[jax.ShapeDtypeStruct((nb * 4, SWA_Q_HEADS * HEAD_DIM), BF16),
                   jax.ShapeDtypeStruct(flat, F32), jax.ShapeDtypeStruct(flat, F32)],
        compiler_params=_params("parallel"),
    )(proj, proj, proj, k_cache.reshape(flat), v_cache.reshape(flat), b_old, b_new,
      q_gain.reshape(1, HEAD_DIM), k_gain.reshape(1, HEAD_DIM), sink_row)


MOE_TM = 512
PACK_W = D_MODEL // 2
HI_MASK = np.uint32(0xFFFF0000)


def _router_kernel(y_ref, g_ref, wr_ref, hp_ref, idx_ref, gate_ref):
    h = _rms(y_ref[...], g_ref[...])
    logits = _dot(h, wr_ref[...], _NN, HIGHEST)
    lane = lax.broadcasted_iota(jnp.int32, logits.shape, 1)
    l1 = jnp.where(lane < N_EXPERTS, logits, NEG_BIG)
    m1 = jnp.max(l1, axis=1, keepdims=True)
    i1 = jnp.min(jnp.where(l1 == m1, lane, HEAD_DIM), axis=1, keepdims=True)
    l2 = jnp.where(lane == i1, NEG_BIG, l1)
    m2 = jnp.max(l2, axis=1, keepdims=True)
    i2 = jnp.min(jnp.where(l2 == m2, lane, HEAD_DIM), axis=1, keepdims=True)
    e = jnp.exp(m2 - m1)
    g1 = 1.0 / (1.0 + e)
    idx_ref[...] = jnp.where(lane == 0, i1, jnp.where(lane == 1, i2, 0))
    gate_ref[...] = jnp.where(lane == 0, g1, jnp.where(lane == 1, e * g1, 0.0))
    bits = lax.bitcast_convert_type(h.astype(BF16).astype(F32), jnp.uint32)
    hp_ref[...] = (bits[:, :PACK_W] & HI_MASK) | (bits[:, PACK_W:] >> 16)


def moe_router(y, gain, router, tm_pref=1088):
    m, k = y.shape
    tm = _tile(m, tm_pref)
    wr = jnp.zeros((k, HEAD_DIM), F32).at[:, :N_EXPERTS].set(router.astype(F32))
    row = lambda w: pl.BlockSpec((tm, w), lambda i: (i, 0))
    return pl.pallas_call(
        _router_kernel,
        grid=(m // tm,),
        in_specs=[row(k), pl.BlockSpec((1, k), lambda i: (0, 0)), pl.BlockSpec((k, HEAD_DIM), lambda i: (0, 0))],
        out_specs=[row(PACK_W), row(HEAD_DIM), row(HEAD_DIM)],
        out_shape=[jax.ShapeDtypeStruct((m, PACK_W), jnp.uint32),
                   jax.ShapeDtypeStruct((m, HEAD_DIM), jnp.int32),
                   jax.ShapeDtypeStruct((m, HEAD_DIM), F32)],
        compiler_params=_params("parallel"),
    )(y, gain.reshape(1, k), wr)


def _route_plan(idx, n_tiles):
    n_pairs = idx.size
    ef = idx.reshape(-1)
    experts = jnp.arange(N_EXPERTS, dtype=jnp.int32)
    onehot = (ef[:, None] == experts[None, :]).astype(jnp.int32)
    csum = jnp.cumsum(onehot, axis=0)
    counts = csum[-1]
    padded = ((counts + MOE_TM - 1) // MOE_TM) * MOE_TM
    ends = jnp.cumsum(padded)
    pos = jnp.sum(onehot * (csum - 1 + (ends - padded)[None, :]), axis=1).astype(jnp.int32)
    n_valid = (ends[-1] // MOE_TM).astype(jnp.int32).reshape(1)
    tile_start = jnp.arange(n_tiles, dtype=jnp.int32) * MOE_TM
    tile_expert = jnp.sum(tile_start[:, None] >= ends[None, :], axis=1).astype(jnp.int32)
    last_used = jnp.max(jnp.where(padded > 0, experts, 0))
    tile_expert = jnp.minimum(tile_expert, last_used)
    src_tok = jnp.zeros((n_tiles * MOE_TM,), jnp.int32).at[pos].set(jnp.arange(n_pairs, dtype=jnp.int32) // 2)
    return pos, src_tok, tile_expert, n_valid


def _gather_kernel(tok_ref, src_ref, dst_ref, sem):
    rows = dst_ref.shape[0]
    base = pl.program_id(0) * rows

    def copy(r):
        return pltpu.make_async_copy(src_ref.at[pl.ds(tok_ref[base + r], 1)], dst_ref.at[pl.ds(r, 1)], sem)

    def start(r, c):
        copy(r).start()
        return c

    def wait(r, c):
        copy(r).wait()
        return c

    lax.fori_loop(0, rows, start, 0)
    lax.fori_loop(0, rows, wait, 0)


def gather_rows(src, src_tok):
    rows = src_tok.shape[0]
    w = src.shape[1]
    return pl.pallas_call(
        _gather_kernel,
        grid_spec=pltpu.PrefetchScalarGridSpec(
            num_scalar_prefetch=1,
            grid=(rows // MOE_TM,),
            in_specs=[pl.BlockSpec(memory_space=pl.ANY)],
            out_specs=pl.BlockSpec((MOE_TM, w), lambda i, tok: (i, 0)),
            scratch_shapes=[pltpu.SemaphoreType.DMA(())]),
        out_shape=jax.ShapeDtypeStruct((rows, w), src.dtype),
        compiler_params=_params("arbitrary"),
    )(src_tok, src)


def _unpack_bf16(words):
    hi = lax.bitcast_convert_type(words & HI_MASK, F32).astype(BF16)
    lo = lax.bitcast_convert_type(words << 16, F32).astype(BF16)
    return hi, lo


def _expert_changed(te_ref):
    i = pl.program_id(1)
    return (i == 0) | (te_ref[i] != te_ref[jnp.maximum(i - 1, 0)])


def _moe_gate_up_kernel(te_ref, nv_ref, x_ref, wg_ref, wu_ref, o_ref, wgb, wub):
    i = pl.program_id(1)

    @pl.when(_expert_changed(te_ref))
    def _():
        wgb[...] = wg_ref[0].astype(BF16)
        wub[...] = wu_ref[0].astype(BF16)

    @pl.when(i < nv_ref[0])
    def _():
        hi, lo = _unpack_bf16(x_ref[...])
        g = _dot(hi, wgb[:PACK_W], _NN) + _dot(lo, wgb[PACK_W:], _NN)
        u = _dot(hi, wub[:PACK_W], _NN) + _dot(lo, wub[PACK_W:], _NN)
        o_ref[...] = (_silu(g) * u).astype(o_ref.dtype)

    @pl.when(i >= nv_ref[0])
    def _():
        o_ref[...] = jnp.zeros_like(o_ref)


def moe_gate_up(xs, w_gu, tile_expert, n_valid, tf=1024):
    rows = xs.shape[0]
    n_tiles = rows // MOE_TM
    off = D_FF // tf
    return pl.pallas_call(
        _moe_gate_up_kernel,
        grid_spec=pltpu.PrefetchScalarGridSpec(
            num_scalar_prefetch=2,
            grid=(D_FF // tf, n_tiles),
            in_specs=[pl.BlockSpec((MOE_TM, PACK_W), lambda j, i, te, nv: (i, 0)),
                      pl.BlockSpec((1, D_MODEL, tf), lambda j, i, te, nv: (te[i], 0, j)),
                      pl.BlockSpec((1, D_MODEL, tf), lambda j, i, te, nv: (te[i], 0, j + off))],
            out_specs=pl.BlockSpec((MOE_TM, tf), lambda j, i, te, nv: (i, j)),
            scratch_shapes=[pltpu.VMEM((D_MODEL, tf), BF16), pltpu.VMEM((D_MODEL, tf), BF16)]),
        out_shape=jax.ShapeDtypeStruct((rows, D_FF), BF16),
        compiler_params=_params("parallel", "arbitrary"),
    )(tile_expert, n_valid, xs, w_gu, w_gu)


def _moe_down_kernel(te_ref, nv_ref, a_ref, w_ref, o_ref, wb):
    i = pl.program_id(1)

    @pl.when(_expert_changed(te_ref))
    def _():
        wb[...] = w_ref[0].astype(BF16)

    @pl.when(i < nv_ref[0])
    def _():
        o_ref[...] = _dot(a_ref[...], wb[...], _NN)

    @pl.when(i >= nv_ref[0])
    def _():
        o_ref[...] = jnp.zeros_like(o_ref)


def moe_down(act, w_down, tile_expert, n_valid, tn=512):
    rows = act.shape[0]
    n_tiles = rows // MOE_TM
    return pl.pallas_call(
        _moe_down_kernel,
        grid_spec=pltpu.PrefetchScalarGridSpec(
            num_scalar_prefetch=2,
            grid=(D_MODEL // tn, n_tiles),
            in_specs=[pl.BlockSpec((MOE_TM, D_FF), lambda j, i, te, nv: (i, 0)),
                      pl.BlockSpec((1, D_FF, tn), lambda j, i, te, nv: (te[i], 0, j))],
            out_specs=pl.BlockSpec((MOE_TM, tn), lambda j, i, te, nv: (i, j)),
            scratch_shapes=[pltpu.VMEM((D_FF, tn), BF16)]),
        out_shape=jax.ShapeDtypeStruct((rows, D_MODEL), F32),
        compiler_params=_params("parallel", "arbitrary"),
    )(tile_expert, n_valid, act, w_down)


COMBINE_TOKENS = 256


def _combine_kernel(pos_ref, y_ref, gate_ref, src_ref, op_ref, os_ref, buf, sem, *, prompt_steps):
    i = pl.program_id(0)
    base = i * COMBINE_TOKENS

    def copy(r, k):
        return pltpu.make_async_copy(src_ref.at[pl.ds(pos_ref[2 * (base + r) + k], 1)], buf.at[k, pl.ds(r, 1)], sem)

    def start(r, c):
        copy(r, 0).start()
        copy(r, 1).start()
        return c

    def wait(r, c):
        copy(r, 0).wait()
        copy(r, 1).wait()
        return c

    lax.fori_loop(0, COMBINE_TOKENS, start, 0)
    lax.fori_loop(0, COMBINE_TOKENS, wait, 0)
    gate = gate_ref[...]
    res = y_ref[...] + gate[:, 0:1] * buf[0] + gate[:, 1:2] * buf[1]

    @pl.when(i < prompt_steps)
    def _():
        op_ref[...] = res

    @pl.when(i >= prompt_steps)
    def _():
        os_ref[...] = res


def moe_combine(y, gates, o_sorted, pos, n_prompt_rows):
    m, d = y.shape
    tc = COMBINE_TOKENS
    ps = n_prompt_rows // tc
    return pl.pallas_call(
        functools.partial(_combine_kernel, prompt_steps=ps),
        grid_spec=pltpu.PrefetchScalarGridSpec(
            num_scalar_prefetch=1,
            grid=(m // tc,),
            in_specs=[pl.BlockSpec((tc, d), lambda i, pos: (i, 0)),
                      pl.BlockSpec((tc, HEAD_DIM), lambda i, pos: (i, 0)),
                      pl.BlockSpec(memory_space=pl.ANY)],
            out_specs=[pl.BlockSpec((tc, d), lambda i, pos: (jnp.minimum(i, ps - 1), 0)),
                       pl.BlockSpec((tc, d), lambda i, pos: (jnp.maximum(i - ps, 0), 0))],
            scratch_shapes=[pltpu.VMEM((2, tc, d), F32), pltpu.SemaphoreType.DMA(())]),
        out_shape=[jax.ShapeDtypeStruct((n_prompt_rows, d), F32), jax.ShapeDtypeStruct((m - n_prompt_rows, d), F32)],
        compiler_params=_params("arbitrary"),
    )(pos, y, gates, o_sorted)


def _gate_layout(ab, rows):
    nq = GDN_HEADS // GDN_HG
    a = ab[:, :GDN_HEADS].reshape(rows // GDN_CHUNK, GDN_CHUNK, nq, GDN_HG)
    b = ab[:, GDN_HEADS:].reshape(rows // GDN_CHUNK, GDN_CHUNK, nq, GDN_HG)
    return jnp.transpose(jnp.concatenate([a, b], axis=3), (0, 2, 3, 1))


def _memory_kv(mem_rows, mem_gain, w_mkv, layer, k_gain):
    h = rmsnorm_rows(mem_rows, mem_gain)
    kv = matmul_ws(h, w_mkv, layer=layer, n_cols=2 * XQ_DIM, tn=512, tm_pref=512)
    return mem_kv_split(kv, k_gain)


def _last_rows(proj, n_seq, seq_len, n_rows, c0, c1):
    return jnp.stack([proj[(n + 1) * seq_len - n_rows:(n + 1) * seq_len, c0:c1] for n in range(n_seq)])


def kernel(x_prompt, x_sample, state_gdn, state_gdn_conv, cache_swa_k, cache_swa_v, cache_mem_k, cache_mem_v,
           mem_prompt, rel_bias, a_norm, a_w_in, a_conv, a_A_log, a_dt_bias, a_o_norm, a_w_out,
           b_norm, b_w_in, b_q_norm, b_k_norm, b_sinks, b_w_out, c_mem_norm, c_w_mkv, c_q_norm, c_k_norm,
           f_norm, d_w_gate_up, d_w_down, m_router, m_w_gate_up, m_w_down):
    n_p, seq, d = x_prompt.shape
    n_s, dec = x_sample.shape[:2]
    t_p = n_p * seq
    t_s = n_s * dec
    t = t_p + t_s
    n_mem = mem_prompt.shape[1]
    x = (x_prompt.reshape(t_p, d).astype(F32), x_sample.reshape(t_s, d).astype(F32))
    mem_rows = mem_prompt.reshape(n_p * n_mem, d).astype(F32)

    mk0, mv0 = _memory_kv(mem_rows, c_mem_norm[0], c_w_mkv, 0, c_k_norm[0])
    h = rmsnorm_rows(x, a_norm[0])
    o2 = GDN_CONV_DIM + GDN_QK
    o4 = o2 + 2 * GDN_HEADS
    proj = matmul_ws(h, a_w_in, n_cols=o2, tn=512, tm_pref=1088)
    aux_w = 5 * HEAD_DIM
    w_aux = jnp.concatenate([a_w_in[:, :, o4:], a_w_in[:, :, o2:o4],
                             jnp.zeros((1, d, aux_w - XQ_DIM - 2 * GDN_HEADS), F32)], axis=2)
    proj_aux = matmul_ws(h, w_aux, n_cols=aux_w, tn=aux_w, tm_pref=1088)
    gt = _gate_layout(proj_aux[:, XQ_DIM:XQ_DIM + 2 * GDN_HEADS], t)
    prm = gdn_gate_params(a_A_log[0], a_dt_bias[0])
    o_gdn_p, s_p = gdn_prompt(proj, gt, prm, a_conv[0], a_o_norm[0], n_p, seq)
    o_gdn_s, s_s = gdn_sample(proj, t_p, gt, prm, a_conv[0], a_o_norm[0], state_gdn_conv[0], state_gdn[0])
    o_mem_p = xattn_prompt(proj_aux, 0, mk0.reshape(n_p, n_mem, XQ_DIM), mv0.reshape(n_p, n_mem, XQ_DIM),
                           c_q_norm[0], n_p, seq)
    o_mem_s = xattn_sample(proj_aux, t_p, 0, cache_mem_k, cache_mem_v, 0, c_q_norm[0])
    y = matmul_ws([(o_gdn_p, o_gdn_s), (o_mem_p, o_mem_s)], a_w_out, n_cols=d, tn=512, tm_pref=512, residual=x)
    h = rmsnorm_rows(y, f_norm[0])
    act = matmul_ws(h, d_w_gate_up, n_cols=D_FF, tn=512, tm_pref=1088, out_dtype=BF16, up_col_offset=D_FF)
    y = matmul_ws(act, d_w_down, n_cols=d, tn=512, tm_pref=544, residual=y, weight_buffers=1)
    conv_p = _last_rows(proj, n_p, seq, GDN_CONV - 1, 0, GDN_CONV_DIM)
    conv_s = proj[t_p:, :GDN_CONV_DIM].reshape(n_s, dec, GDN_CONV_DIM)[:, dec - (GDN_CONV - 1):]

    mk1, mv1 = _memory_kv(mem_rows, c_mem_norm[1], c_w_mkv, 1, c_k_norm[1])
    h = rmsnorm_rows(y, b_norm[0])
    swa_in = SWA_Q_HEADS * HEAD_DIM + 2 * SWA_KV_HEADS * HEAD_DIM + XQ_DIM
    proj = matmul_ws(h, b_w_in, n_cols=swa_in, tn=512, tm_pref=1088)
    kv_w = SWA_KV_HEADS * HEAD_DIM
    wbuf = cache_swa_k.shape[2]
    o_swa_p, kn_p = swa_prompt(proj, rel_bias, b_q_norm[0], b_k_norm[0], b_sinks[0], n_p, seq)
    o_swa_s, k_new, v_new = swa_sample(proj, t_p, cache_swa_k[0], cache_swa_v[0], rel_bias,
                                       b_q_norm[0], b_k_norm[0], b_sinks[0])
    xq_blk = (swa_in - XQ_DIM) // XQ_DIM
    o_mem_p = xattn_prompt(proj, xq_blk, mk1.reshape(n_p, n_mem, XQ_DIM), mv1.reshape(n_p, n_mem, XQ_DIM),
                           c_q_norm[1], n_p, seq)
    o_mem_s = xattn_sample(proj, t_p, xq_blk, cache_mem_k, cache_mem_v, 1, c_q_norm[1])
    y = matmul_ws([(o_swa_p, o_swa_s), (o_mem_p, o_mem_s)], b_w_out, n_cols=d, tn=512, tm_pref=512, residual=y)
    v_off = SWA_Q_HEADS * HEAD_DIM + kv_w
    wl = min(WINDOW, seq)
    v_p = _last_rows(proj, n_p, seq, wl, v_off, v_off + kv_w)

    hp, idx, gates = moe_router(y, f_norm[1], m_router[0])
    n_tiles = (2 * t) // MOE_TM + N_EXPERTS
    pos, src_tok, tile_expert, n_valid = _route_plan(idx[:, :2], n_tiles)
    xs = gather_rows(hp, src_tok)
    act = moe_gate_up(xs, m_w_gate_up[0], tile_expert, n_valid)
    o_sorted = moe_down(act, m_w_down[0], tile_expert, n_valid)
    y_p, y_s = moe_combine(y, gates, o_sorted, pos, t_p)

    kv_shape = (SWA_KV_HEADS, HEAD_DIM)
    mem_shape = (n_p, n_mem, MEM_HEADS, HEAD_DIM)
    return (y_p.reshape(n_p, seq, d), y_s.reshape(n_s, dec, d),
            s_p[None], s_s[None], conv_p[None], conv_s[None],
            kn_p.reshape((1, n_p, wl) + kv_shape), k_new.reshape((1, n_s, wbuf) + kv_shape),
            v_p.reshape((1, n_p, wl) + kv_shape), v_new.reshape((1, n_s, wbuf) + kv_shape),
            jnp.stack([mk0.reshape(mem_shape), mk1.reshape(mem_shape)]),
            jnp.stack([mv0.reshape(mem_shape), mv1.reshape(mem_shape)]))
```
